```python
import jax, jax.numpy as jnp
from jax import lax
import numpy as np

D_MODEL = 2048
BATCH = 2
SEQ = 8192
DEPTH = 2

N_MIXERS = 2
HEAD_DIM = 128
EPS = 1e-6
FOX_HEADS = D_MODEL // HEAD_DIM
FOX_BLOCK = 128
DIL_PATTERNS = ((128, 1), (512, 4), (2048, 16))
N_GROUPS = len(DIL_PATTERNS)
DIL_SPAN = 128
DIL_HEADS = D_MODEL // (2 * HEAD_DIM)
DIL_V_DIM = D_MODEL // DIL_HEADS
ALIBI_MAX_EXP = 8.0
D_FF = 4 * D_MODEL
N_FOX_LAYERS = (DEPTH + 1) // 2
N_DIL_LAYERS = DEPTH // 2

kernel_name = "fox_dilated_hybrid_trunk"


def rms_norm(x, g):
    xf = x.astype(jnp.float32)
    y = xf * lax.rsqrt(jnp.mean(xf * xf, axis=-1, keepdims=True) + EPS)
    return (y * g.astype(jnp.float32)).astype(x.dtype)


def sq_relu_mlp(h, w_up, w_down):
    a = jax.nn.relu(h @ w_up)
    return (a * a) @ w_down


def fox_attention(h, w_in, b_f, q_gain, k_gain, w_out):
    B, S, _ = h.shape
    H, dh = FOX_HEADS, HEAD_DIM
    proj = h @ w_in
    q, k, v, f = jnp.split(proj, [H * dh, 2 * H * dh, 3 * H * dh], axis=-1)
    q = rms_norm(q.reshape(B, S, H, dh), q_gain)
    k = rms_norm(k.reshape(B, S, H, dh), k_gain)
    v = v.reshape(B, S, H, dh)
    log_f = jax.nn.log_sigmoid((f + b_f).astype(jnp.float32))
    c = jnp.cumsum(log_f, axis=1)
    c_keys = jnp.transpose(c, (0, 2, 1))
    nb = S // FOX_BLOCK
    qb = jnp.moveaxis(q.reshape(B, nb, FOX_BLOCK, H, dh), 1, 0)
    cb = jnp.moveaxis(c.reshape(B, nb, FOX_BLOCK, H), 1, 0)
    kpos = jnp.arange(S)
    scale = dh ** -0.5

    def one_block(args):
        i, q_i, c_i = args
        s = jnp.einsum('bqhd,bkhd->bhqk', q_i, k, preferred_element_type=jnp.float32) * scale
        s = s + jnp.transpose(c_i, (0, 2, 1))[..., None] - c_keys[:, :, None, :]
        qpos = i * FOX_BLOCK + jnp.arange(FOX_BLOCK)
        causal = kpos[None, :] <= qpos[:, None]
        s = jnp.where(causal, s, -jnp.inf)
        p = jax.nn.softmax(s, axis=-1)
        return jnp.einsum('bhqk,bkhd->bqhd', p.astype(v.dtype), v)

    o = lax.map(one_block, (jnp.arange(nb), qb, cb))
    o = jnp.moveaxis(o, 0, 1).reshape(B, S, H * dh)
    return o @ w_out


def dilated_group(q, k, v, slopes, r):
    B, S, H, dh = q.shape
    L = S // r
    nb = -(-L // DIL_SPAN)
    Lp = nb * DIL_SPAN

    def to_blocks(t):
        t = t.reshape((B, L, r) + t.shape[2:])
        t = jnp.moveaxis(t, 2, 1)
        t = jnp.pad(t, [(0, 0), (0, 0), (0, Lp - L)] + [(0, 0)] * (t.ndim - 3))
        return t.reshape((B, r, nb, DIL_SPAN) + t.shape[3:])

    def with_prev(t):
        prev = jnp.pad(t, [(0, 0), (0, 0), (1, 0)] + [(0, 0)] * (t.ndim - 3))[:, :, :-1]
        return jnp.concatenate([prev, t], axis=3)

    def from_blocks(t):
        t = t.reshape((B, r, Lp) + t.shape[4:])[:, :, :L]
        t = jnp.moveaxis(t, 1, 2)
        return t.reshape((B, S) + t.shape[3:])

    qb = to_blocks(q)
    kw = with_prev(to_blocks(k))
    vw = with_prev(to_blocks(v))
    s = jnp.einsum('brnqhd,brnkhd->brnhqk', qb, kw, preferred_element_type=jnp.float32) * dh ** -0.5
    qi = jnp.arange(DIL_SPAN)[:, None]
    kj = jnp.arange(2 * DIL_SPAN)[None, :]
    delta = qi + DIL_SPAN - kj
    blk = jnp.arange(nb)[:, None, None]
    valid = (delta >= 0) & (delta <= DIL_SPAN) & ((blk > 0) | (kj >= DIL_SPAN))
    alibi = -slopes.astype(jnp.float32)[:, None, None] * (delta * r).astype(jnp.float32)
    s = jnp.where(valid[None, None, :, None], s + alibi, -jnp.inf)
    m = jnp.max(s, axis=-1)
    p = jnp.exp(s - m[..., None])
    den = jnp.sum(p, axis=-1)
    num = jnp.einsum('brnhqk,brnkhd->brnqhd', p, vw.astype(jnp.float32))
    return (from_blocks(jnp.swapaxes(m, 3, 4)),
            from_blocks(jnp.swapaxes(den, 3, 4)),
            from_blocks(num))


def dilated_attention(h, w_in, q_gain, k_gain, w_out):
    B, S, _ = h.shape
    G, H, dh, dv = N_GROUPS, DIL_HEADS, HEAD_DIM, DIL_V_DIM
    proj = h @ w_in
    q, k, v = jnp.split(proj, [G * H * dh, 2 * G * H * dh], axis=-1)
    q = rms_norm(q.reshape(B, S, G, H, dh), q_gain[:, None, :])
    k = rms_norm(k.reshape(B, S, G, H, dh), k_gain[:, None, :])
    v = v.reshape(B, S, H, dv)
    slopes = jnp.exp2(-ALIBI_MAX_EXP * jnp.arange(1, G * H + 1, dtype=jnp.float32) / (G * H)).reshape(G, H)
    ms, dens, nums = [], [], []
    for g, (window, r) in enumerate(DIL_PATTERNS):
        m_g, den_g, num_g = dilated_group(q[:, :, g], k[:, :, g], v, slopes[g], r)
        ms.append(m_g); dens.append(den_g); nums.append(num_g)
    m_all = jnp.stack(ms, 0)
    w = jnp.exp(m_all - jnp.max(m_all, axis=0, keepdims=True))
    den = jnp.sum(w * jnp.stack(dens, 0), axis=0)
    num = jnp.sum(w[..., None] * jnp.stack(nums, 0), axis=0)
    o = (num / den[..., None]).reshape(B, S, H * dv).astype(h.dtype)
    return o @ w_out


def setup_inputs(seed: int = 0) -> dict:
    key = jax.random.key(seed)
    ks = jax.random.split(key, 16)
    D = D_MODEL
    nrm = lambda k, shape, fan_in: jax.random.normal(k, shape, jnp.float32) * fan_in ** -0.5
    x = jax.random.normal(ks[0], (BATCH, SEQ, D), jnp.float32)
    fox_qkv = nrm(ks[1], (N_FOX_LAYERS, D, 3 * FOX_HEADS * HEAD_DIM), D)
    fox_fg = 0.1 * nrm(ks[2], (N_FOX_LAYERS, D, FOX_HEADS), D)
    fox_w_in = jnp.concatenate([fox_qkv, fox_fg], axis=-1)
    fox_b_f = 3.0 + 0.1 * jax.random.normal(ks[3], (N_FOX_LAYERS, FOX_HEADS), jnp.float32)
    fox_q_gain = 1.0 + 0.02 * jax.random.normal(ks[4], (N_FOX_LAYERS, HEAD_DIM), jnp.float32)
    fox_k_gain = 1.0 + 0.02 * jax.random.normal(ks[5], (N_FOX_LAYERS, HEAD_DIM), jnp.float32)
    fox_w_out = nrm(ks[6], (N_FOX_LAYERS, FOX_HEADS * HEAD_DIM, D), FOX_HEADS * HEAD_DIM)
    dil_cols = 2 * N_GROUPS * DIL_HEADS * HEAD_DIM + DIL_HEADS * DIL_V_DIM
    dil_w_in = nrm(ks[7], (N_DIL_LAYERS, D, dil_cols), D)
    dil_q_gain = 1.0 + 0.02 * jax.random.normal(ks[8], (N_DIL_LAYERS, N_GROUPS, HEAD_DIM), jnp.float32)
    dil_k_gain = 1.0 + 0.02 * jax.random.normal(ks[9], (N_DIL_LAYERS, N_GROUPS, HEAD_DIM), jnp.float32)
    dil_w_out = nrm(ks[10], (N_DIL_LAYERS, DIL_HEADS * DIL_V_DIM, D), DIL_HEADS * DIL_V_DIM)
    mix_norm_g = 1.0 + 0.02 * jax.random.normal(ks[11], (DEPTH, D), jnp.float32)
    mlp_norm_g = 1.0 + 0.02 * jax.random.normal(ks[12], (DEPTH, D), jnp.float32)
    mlp_w_up = nrm(ks[13], (DEPTH, D, D_FF), D)
    mlp_w_down = nrm(ks[14], (DEPTH, D_FF, D), D_FF)
    return {"x": x, "fox_w_in": fox_w_in, "fox_b_f": fox_b_f, "fox_q_gain": fox_q_gain,
            "fox_k_gain": fox_k_gain, "fox_w_out": fox_w_out, "dil_w_in": dil_w_in,
            "dil_q_gain": dil_q_gain, "dil_k_gain": dil_k_gain, "dil_w_out": dil_w_out,
            "mix_norm_g": mix_norm_g, "mlp_norm_g": mlp_norm_g,
            "mlp_w_up": mlp_w_up, "mlp_w_down": mlp_w_down}


def reference(x, fox_w_in, fox_b_f, fox_q_gain, fox_k_gain, fox_w_out, dil_w_in,
              dil_q_gain, dil_k_gain, dil_w_out, mix_norm_g, mlp_norm_g, mlp_w_up, mlp_w_down):
    for i in range(DEPTH):
        j = i // N_MIXERS
        h = rms_norm(x, mix_norm_g[i])
        if i % N_MIXERS == 0:
            mix = fox_attention(h, fox_w_in[j], fox_b_f[j], fox_q_gain[j], fox_k_gain[j], fox_w_out[j])
        else:
            mix = dilated_attention(h, dil_w_in[j], dil_q_gain[j], dil_k_gain[j], dil_w_out[j])
        x = x + mix.astype(x.dtype)
        h = rms_norm(x, mlp_norm_g[i])
        x = x + sq_relu_mlp(h, mlp_w_up[i], mlp_w_down[i]).astype(x.dtype)
    return x
```

```python
import functools

import numpy as np
import jax
import jax.numpy as jnp
from jax import lax
from jax.experimental import pallas as pl
from jax.experimental.pallas import tpu as pltpu

f32 = jnp.float32
bf16 = jnp.bfloat16

BATCH = 2
SEQ = 8192
D_MODEL = 2048
TOKENS = BATCH * SEQ
HEAD_DIM = 128
EPS = 1e-6
D_FF = 4 * D_MODEL
FOX_HEADS = D_MODEL // HEAD_DIM
DIL_PATTERNS = ((128, 1), (512, 4), (2048, 16))
N_GROUPS = len(DIL_PATTERNS)
DIL_SPAN = 128
DIL_HEADS = D_MODEL // (2 * HEAD_DIM)
DIL_V_DIM = D_MODEL // DIL_HEADS
ALIBI_MAX_EXP = 8.0

LOG2E = 1.4426950408889634
QK_SCALE = HEAD_DIM ** -0.5
MASKED = 1e30
MIB = 1024 * 1024

DIL_TILE = 2048
DIL_RES = 16
DIL_ROWS = DIL_TILE // DIL_RES


def _params(semantics, vmem_mib):
    return pltpu.CompilerParams(dimension_semantics=semantics, vmem_limit_bytes=vmem_mib * MIB)


def _rmsnorm_kernel(x_ref, g_ref, o_ref):
    x = x_ref[...]
    ms = jnp.mean(x * x, axis=-1, keepdims=True)
    o_ref[...] = (x * lax.rsqrt(ms + EPS) * g_ref[...]).astype(o_ref.dtype)


def rmsnorm(x2d, g):
    bm = 512
    return pl.pallas_call(
        _rmsnorm_kernel,
        out_shape=jax.ShapeDtypeStruct((TOKENS, D_MODEL), bf16),
        grid=(TOKENS // bm,),
        in_specs=[pl.BlockSpec((bm, D_MODEL), lambda i: (i, 0)),
                  pl.BlockSpec((1, D_MODEL), lambda i: (0, 0))],
        out_specs=pl.BlockSpec((bm, D_MODEL), lambda i: (i, 0)),
        compiler_params=_params(("arbitrary",), 32),
        name="rmsnorm",
    )(x2d, g.reshape(1, D_MODEL))


def _head_rmsnorm(y, gain):
    ms = jnp.mean(y * y, axis=-1, keepdims=True)
    return y * lax.rsqrt(ms + EPS) * gain


GATE_BM = 512


def _gate_kernel(h_ref, wf_ref, bf_ref, hi_ref, mid_ref, lo_ref, carry_ref):
    @pl.when(pl.program_id(1) == 0)
    def _():
        carry_ref[...] = jnp.zeros_like(carry_ref)

    f = jnp.dot(h_ref[0], wf_ref[...], preferred_element_type=f32) + bf_ref[...]
    lf = jnp.minimum(f, 0.0) - jnp.log1p(jnp.exp(-jnp.abs(f)))
    row = lax.broadcasted_iota(jnp.int32, (GATE_BM, GATE_BM), 0)
    col = lax.broadcasted_iota(jnp.int32, (GATE_BM, GATE_BM), 1)
    tri = (col <= row).astype(bf16)
    p0 = lf.astype(bf16)
    r0 = lf - p0.astype(f32)
    p1 = r0.astype(bf16)
    p2 = (r0 - p1.astype(f32)).astype(bf16)
    cs = (jnp.dot(tri, p0, preferred_element_type=f32)
          + jnp.dot(tri, p1, preferred_element_type=f32)
          + jnp.dot(tri, p2, preferred_element_type=f32))
    c = cs + carry_ref[...]
    carry_ref[...] = c[GATE_BM - 1:GATE_BM, :]
    c2 = c * LOG2E
    hi = c2.astype(bf16)
    r1 = c2 - hi.astype(f32)
    mid = r1.astype(bf16)
    lo = (r1 - mid.astype(f32)).astype(bf16)
    hi_ref[0] = hi
    mid_ref[0] = mid
    lo_ref[0] = lo


def fox_gate(h3, wf, bfp):
    piece = jax.ShapeDtypeStruct((BATCH, SEQ, HEAD_DIM), bf16)
    spec = pl.BlockSpec((1, GATE_BM, HEAD_DIM), lambda b, s: (b, s, 0))
    return pl.pallas_call(
        _gate_kernel,
        out_shape=(piece, piece, piece),
        grid=(BATCH, SEQ // GATE_BM),
        in_specs=[pl.BlockSpec((1, GATE_BM, D_MODEL), lambda b, s: (b, s, 0)),
                  pl.BlockSpec((D_MODEL, HEAD_DIM), lambda b, s: (0, 0)),
                  pl.BlockSpec((1, HEAD_DIM), lambda b, s: (0, 0))],
        out_specs=(spec, spec, spec),
        scratch_shapes=[pltpu.VMEM((1, HEAD_DIM), f32)],
        compiler_params=_params(("arbitrary", "arbitrary"), 32),
        name="fox_gate",
    )(h3, wf, bfp)


PROJ_BM = 1024
PROJ_BN = 1024
PROJ_HEADS = PROJ_BN // HEAD_DIM
MXU_N = 256
FOX_BK = 512


def _fox_proj_kernel(h_ref, w_ref, g_ref, o_ref, *, mode):
    h = h_ref[0]
    for s in range(PROJ_BN // MXU_N):
        y2 = jnp.dot(h, w_ref[:, s * MXU_N:(s + 1) * MXU_N], preferred_element_type=f32)
        for u in range(MXU_N // HEAD_DIM):
            hh = s * (MXU_N // HEAD_DIM) + u
            y = y2[:, u * HEAD_DIM:(u + 1) * HEAD_DIM]
            if mode == "q":
                o_ref[0, hh] = _head_rmsnorm(y, g_ref[...]).T.astype(bf16)
            elif mode == "k":
                o_ref[0, hh] = _head_rmsnorm(y, g_ref[...]).astype(bf16)
            else:
                for c in range(PROJ_BM // FOX_BK):
                    o_ref[0, hh, c] = y[c * FOX_BK:(c + 1) * FOX_BK, :].T.astype(bf16)


def fox_proj(h3, w, gain, mode, col_block0):
    nsb = SEQ // PROJ_BM
    if mode == "q":
        shape = (BATCH, FOX_HEADS, HEAD_DIM, SEQ)
        ospec = pl.BlockSpec((1, PROJ_HEADS, HEAD_DIM, PROJ_BM), lambda b, s, j: (b, j, 0, s))
    elif mode == "k":
        shape = (BATCH, FOX_HEADS, SEQ, HEAD_DIM)
        ospec = pl.BlockSpec((1, PROJ_HEADS, PROJ_BM, HEAD_DIM), lambda b, s, j: (b, j, s, 0))
    else:
        shape = (BATCH, FOX_HEADS, SEQ // FOX_BK, HEAD_DIM, FOX_BK)
        ospec = pl.BlockSpec((1, PROJ_HEADS, PROJ_BM // FOX_BK, HEAD_DIM, FOX_BK),
                             lambda b, s, j: (b, j, s, 0, 0))
    return pl.pallas_call(
        functools.partial(_fox_proj_kernel, mode=mode),
        out_shape=jax.ShapeDtypeStruct(shape, bf16),
        grid=(BATCH, nsb, D_MODEL // PROJ_BN),
        in_specs=[pl.BlockSpec((1, PROJ_BM, D_MODEL), lambda b, s, j: (b, s, 0)),
                  pl.BlockSpec((D_MODEL, PROJ_BN), lambda b, s, j: (0, col_block0 + j)),
                  pl.BlockSpec((1, HEAD_DIM), lambda b, s, j: (0, 0))],
        out_specs=ospec,
        compiler_params=_params(("arbitrary",) * 3, 48),
        name="fox_proj_" + mode,
    )(h3, w, gain)


FOX_BQ = 512
AUG = 2 * HEAD_DIM


def _fox_attn_kernel(qT_ref, qaT_ref, k_ref, ka_ref, vT_ref, o_ref, kaug_ref, m_ref, l_ref, acc_ref):
    qi = pl.program_id(2)

    @pl.when(qi == 0)
    def _():
        kaug_ref[:, :HEAD_DIM] = k_ref[0, 0]
        kaug_ref[:, HEAD_DIM:] = ka_ref[0, 0]

    q_aug = jnp.concatenate([qT_ref[0, 0], qaT_ref[0, 0]], axis=0)
    m_ref[...] = jnp.full_like(m_ref, -MASKED)
    l_ref[...] = jnp.zeros_like(l_ref)
    acc_ref[...] = jnp.zeros_like(acc_ref)

    def step(kb, diagonal):
        start = pl.multiple_of(kb * FOX_BK, FOX_BK)
        s = jnp.dot(kaug_ref[pl.ds(start, FOX_BK), :], q_aug, preferred_element_type=f32)
        if diagonal:
            kk = lax.broadcasted_iota(jnp.int32, s.shape, 0)
            qq = lax.broadcasted_iota(jnp.int32, s.shape, 1)
            s = jnp.where(kk <= qq, s, -MASKED)
        m_prev = m_ref[...]
        m_new = jnp.maximum(m_prev, jnp.max(s, axis=0, keepdims=True))
        alpha = jnp.exp2(m_prev - m_new)
        p = jnp.exp2(s - m_new)
        l_ref[...] = alpha * l_ref[...] + jnp.sum(p, axis=0, keepdims=True)
        pv = jnp.dot(vT_ref[0, 0, kb], p.astype(bf16), preferred_element_type=f32)
        acc_ref[...] = alpha * acc_ref[...] + pv
        m_ref[...] = m_new

    def body(kb, carry):
        step(kb, False)
        return carry

    lax.fori_loop(0, qi, body, 0)
    step(qi, True)
    o = acc_ref[...] / l_ref[...]
    o_ref[0] = o.T.astype(bf16)


def fox_attention(qT, qaT, k, ka, vT):
    nq = SEQ // FOX_BQ
    nk = SEQ // FOX_BK
    qspec = pl.BlockSpec((1, 1, HEAD_DIM, FOX_BQ), lambda b, h, i: (b, h, 0, i))
    kspec = pl.BlockSpec((1, 1, SEQ, HEAD_DIM), lambda b, h, i: (b, h, 0, 0))
    return pl.pallas_call(
        _fox_attn_kernel,
        out_shape=jax.ShapeDtypeStruct((BATCH, SEQ, D_MODEL), bf16),
        grid=(BATCH, FOX_HEADS, nq),
        in_specs=[qspec, qspec, kspec, kspec,
                  pl.BlockSpec((1, 1, nk, HEAD_DIM, FOX_BK), lambda b, h, i: (b, h, 0, 0, 0))],
        out_specs=pl.BlockSpec((1, FOX_BQ, HEAD_DIM), lambda b, h, i: (b, i, h)),
        scratch_shapes=[pltpu.VMEM((SEQ, AUG), bf16),
                        pltpu.VMEM((1, FOX_BQ), f32),
                        pltpu.VMEM((1, FOX_BQ), f32),
                        pltpu.VMEM((HEAD_DIM, FOX_BQ), f32)],
        compiler_params=_params(("arbitrary",) * 3, 40),
        name="fox_attention",
    )(qT, qaT, k, ka, vT)


OUT_BM = 512
OUT_BN = 1024


def _outproj_kernel(o_ref, w_ref, x_ref, y_ref):
    y_ref[...] = x_ref[...] + jnp.dot(o_ref[...], w_ref[...], preferred_element_type=f32)


def outproj_residual(o2d, w, x2d):
    return pl.pallas_call(
        _outproj_kernel,
        out_shape=jax.ShapeDtypeStruct((TOKENS, D_MODEL), f32),
        grid=(TOKENS // OUT_BM, D_MODEL // OUT_BN),
        in_specs=[pl.BlockSpec((OUT_BM, D_MODEL), lambda i, j: (i, 0)),
                  pl.BlockSpec((D_MODEL, OUT_BN), lambda i, j: (0, j)),
                  pl.BlockSpec((OUT_BM, OUT_BN), lambda i, j: (i, j))],
        out_specs=pl.BlockSpec((OUT_BM, OUT_BN), lambda i, j: (i, j)),
        compiler_params=_params(("arbitrary", "arbitrary"), 40),
        name="outproj_residual",
    )(o2d, w, x2d)


MLP_BM = 512
MLP_BF = 512


def _mlp_kernel(h_ref, wu_ref, wd_ref, x_ref, y_ref, acc_ref):
    f = pl.program_id(1)

    @pl.when(f == 0)
    def _():
        acc_ref[...] = jnp.zeros_like(acc_ref)

    a = jnp.maximum(jnp.dot(h_ref[...], wu_ref[...], preferred_element_type=f32), 0.0)
    acc_ref[...] += jnp.dot((a * a).astype(bf16), wd_ref[...], preferred_element_type=f32)

    @pl.when(f == pl.num_programs(1) - 1)
    def _():
        y_ref[...] = x_ref[...] + acc_ref[...]


def mlp_residual(h2d, w_up, w_down, x2d):
    return pl.pallas_call(
        _mlp_kernel,
        out_shape=jax.ShapeDtypeStruct((TOKENS, D_MODEL), f32),
        grid=(TOKENS // MLP_BM, D_FF // MLP_BF),
        in_specs=[pl.BlockSpec((MLP_BM, D_MODEL), lambda i, f: (i, 0)),
                  pl.BlockSpec((D_MODEL, MLP_BF), lambda i, f: (0, f)),
                  pl.BlockSpec((MLP_BF, D_MODEL), lambda i, f: (f, 0)),
                  pl.BlockSpec((MLP_BM, D_MODEL), lambda i, f: (i, 0))],
        out_specs=pl.BlockSpec((MLP_BM, D_MODEL), lambda i, f: (i, 0)),
        scratch_shapes=[pltpu.VMEM((MLP_BM, D_MODEL), f32)],
        compiler_params=_params(("arbitrary", "arbitrary"), 48),
        name="mlp_residual",
    )(h2d, w_up, w_down, x2d)


DIL_COLS = 2 * N_GROUPS * DIL_HEADS * HEAD_DIM + DIL_HEADS * DIL_V_DIM
DIL_RPB = PROJ_BM // DIL_ROWS
DIL_QK_BLOCKS = 2 * N_GROUPS


def _residue_view(a2d):
    return a2d.reshape(TOKENS // DIL_TILE, DIL_ROWS, DIL_RES * a2d.shape[1])


def _dil_proj_kernel(*refs):
    h_refs = refs[:DIL_RPB]
    w_ref, g_ref, o_ref = refs[DIL_RPB:]
    j = pl.program_id(1)

    @pl.when(j < DIL_QK_BLOCKS)
    def _():
        for r in range(DIL_RPB):
            y = jnp.dot(h_refs[r][0], w_ref[...], preferred_element_type=f32)
            for hh in range(PROJ_HEADS):
                cols = slice(hh * HEAD_DIM, (hh + 1) * HEAD_DIM)
                o_ref[r * DIL_ROWS:(r + 1) * DIL_ROWS, cols] = _head_rmsnorm(y[:, cols], g_ref[0]).astype(bf16)

    @pl.when(j >= DIL_QK_BLOCKS)
    def _():
        for r in range(DIL_RPB):
            y = jnp.dot(h_refs[r][0], w_ref[...], preferred_element_type=f32)
            o_ref[r * DIL_ROWS:(r + 1) * DIL_ROWS, :] = y.astype(bf16)


def dil_proj(h2d, w, gains):
    halves = DIL_TILE // PROJ_BM
    hspecs = [pl.BlockSpec((1, DIL_ROWS, D_MODEL), lambda i, j, r=r: (i // halves, 0, (i % halves) * DIL_RPB + r))
              for r in range(DIL_RPB)]
    out = pl.pallas_call(
        _dil_proj_kernel,
        out_shape=jax.ShapeDtypeStruct((TOKENS, DIL_COLS), bf16),
        grid=(TOKENS // PROJ_BM, DIL_COLS // PROJ_BN),
        in_specs=hspecs + [pl.BlockSpec((D_MODEL, PROJ_BN), lambda i, j: (0, j)),
                           pl.BlockSpec((1, 1, HEAD_DIM), lambda i, j: (j, 0, 0))],
        out_specs=pl.BlockSpec((PROJ_BM, PROJ_BN), lambda i, j: (i, j)),
        compiler_params=_params(("arbitrary", "arbitrary"), 48),
        name="dil_proj",
    )(*([_residue_view(h2d)] * DIL_RPB), w, gains)
    return out.reshape(BATCH, SEQ, DIL_COLS)


def _dil_distance_tables():
    def table(delta, prev_half, r):
        valid = (delta >= 0) & (delta <= DIL_SPAN)
        d = np.where(valid, delta * (r * LOG2E), MASKED).astype(np.float32)
        first = np.where(prev_half, np.float32(MASKED), d).astype(np.float32)
        return d, first

    i = np.arange(128)[:, None]
    j = np.arange(256)[None, :]
    t16 = table(128 + i - j, j < 128, 16)
    c, u = i // 32, i % 32
    cp, half, up = j // 64, (j % 64) // 32, j % 32
    t4 = table(128 + 4 * u + c - 128 * half - 4 * up - cp, half == 0, 4)
    i = np.arange(256)[:, None]
    j = np.arange(512)[None, :]
    rho, u = i // 16, i % 16
    rp, half, up = j // 32, (j % 32) // 16, j % 16
    t1 = table(256 + 16 * u + rho - 256 * half - 16 * up - rp, half == 0, 1)
    return t1, t4, t16


def _dil_attn_kernel(slopes_ref, q0_ref, q1_ref, q2_ref, k0_ref, k1_ref, k2_ref, v_ref,
                     d1_ref, d1f_ref, d4_ref, d4f_ref, d16_ref, d16f_ref,
                     o_ref, m_ref, l_ref, acc_ref):
    h = pl.program_id(1)
    t = pl.program_id(2)
    base = pl.multiple_of(t * DIL_TILE, DIL_TILE)
    pbase = pl.multiple_of(jnp.maximum(t - 1, 0) * DIL_TILE, DIL_TILE)
    first = t == 0

    def attend(q, kw, vw, bias, rows, size, init):
        s = lax.dot_general(q, kw, (((1,), (1,)), ((), ())), preferred_element_type=f32) + bias
        m_blk = jnp.max(s, axis=1, keepdims=True)
        if init:
            m_new = m_blk
        else:
            m_old = jnp.concatenate([m_ref[r:r + size, :] for r in rows], axis=0)
            l_old = jnp.concatenate([l_ref[r:r + size, :] for r in rows], axis=0)
            o_old = jnp.concatenate([acc_ref[r:r + size, :] for r in rows], axis=0)
            m_new = jnp.maximum(m_old, m_blk)
        p = jnp.exp2(s - m_new)
        l_new = jnp.sum(p, axis=1, keepdims=True)
        o_new = jnp.dot(p.astype(bf16), vw, preferred_element_type=f32)
        if not init:
            alpha = jnp.exp2(m_old - m_new)
            l_new = alpha * l_old + l_new
            o_new = alpha * o_old + o_new
        for c, r in enumerate(rows):
            m_ref[r:r + size, :] = m_new[c * size:(c + 1) * size]
            l_ref[r:r + size, :] = l_new[c * size:(c + 1) * size]
            acc_ref[r:r + size, :] = o_new[c * size:(c + 1) * size]

    def window(ref, rows, size, n):
        if n > 0:
            return jnp.concatenate([ref[0, pl.ds(base + (r - size), 2 * size), :] for r in rows], axis=0)
        parts = []
        for r in rows:
            parts.append(ref[0, pl.ds(pbase + (r + DIL_ROWS - size), size), :])
            parts.append(ref[0, pl.ds(base + r, size), :])
        return jnp.concatenate(parts, axis=0)

    slope = slopes_ref[2 * DIL_HEADS + h]
    bias = -slope * jnp.where(first, d16f_ref[...], d16_ref[...])
    for rho in range(DIL_RES):
        rows = [rho * DIL_ROWS]
        q = q2_ref[0, rows[0]:rows[0] + DIL_ROWS, :]
        attend(q, window(k2_ref, rows, DIL_ROWS, 0), window(v_ref, rows, DIL_ROWS, 0),
               bias, rows, DIL_ROWS, True)

    slope = slopes_ref[DIL_HEADS + h]
    bias_in = -slope * d4_ref[...]
    bias_edge = -slope * jnp.where(first, d4f_ref[...], d4_ref[...])
    for rho4 in range(4):
        for n in range(4):
            rows = [(rho4 + 4 * c) * DIL_ROWS + 32 * n for c in range(4)]
            q = jnp.concatenate([q1_ref[0, r:r + 32, :] for r in rows], axis=0)
            attend(q, window(k1_ref, rows, 32, n), window(v_ref, rows, 32, n),
                   bias_edge if n == 0 else bias_in, rows, 32, False)

    slope = slopes_ref[h]
    bias_in = -slope * d1_ref[...]
    bias_edge = -slope * jnp.where(first, d1f_ref[...], d1_ref[...])
    for n in range(DIL_ROWS // 16):
        rows = [rho * DIL_ROWS + 16 * n for rho in range(DIL_RES)]
        q = jnp.concatenate([q0_ref[0, r:r + 16, :] for r in rows], axis=0)
        attend(q, window(k0_ref, rows, 16, n), window(v_ref, rows, 16, n),
               bias_edge if n == 0 else bias_in, rows, 16, False)

    o_ref[0] = (acc_ref[...] / l_ref[...]).astype(bf16)


def dil_attention(p3, slopes):
    tables = [jnp.asarray(a) for pair in _dil_distance_tables() for a in pair]
    gh = DIL_HEADS
    kcol0 = N_GROUPS * gh
    vcol0 = 2 * N_GROUPS * gh * HEAD_DIM // DIL_V_DIM
    qspecs = [pl.BlockSpec((1, DIL_TILE, HEAD_DIM), lambda b, h, t, g=g: (b, t, g * gh + h))
              for g in range(N_GROUPS)]
    kspecs = [pl.BlockSpec((1, SEQ, HEAD_DIM), lambda b, h, t, g=g: (b, 0, kcol0 + g * gh + h))
              for g in range(N_GROUPS)]
    vspec = pl.BlockSpec((1, SEQ, DIL_V_DIM), lambda b, h, t: (b, 0, vcol0 + h))
    tspecs = [pl.BlockSpec(a.shape, lambda b, h, t: (0, 0)) for a in tables]
    return pl.pallas_call(
        _dil_attn_kernel,
        out_shape=jax.ShapeDtypeStruct((BATCH, SEQ, D_MODEL), bf16),
        grid=(BATCH, DIL_HEADS, SEQ // DIL_TILE),
        in_specs=[pl.BlockSpec(memory_space=pltpu.SMEM)] + qspecs + kspecs + [vspec] + tspecs,
        out_specs=pl.BlockSpec((1, DIL_TILE, DIL_V_DIM), lambda b, h, t: (b, t, h)),
        scratch_shapes=[pltpu.VMEM((DIL_TILE, 1), f32),
                        pltpu.VMEM((DIL_TILE, 1), f32),
                        pltpu.VMEM((DIL_TILE, DIL_V_DIM), f32)],
        compiler_params=_params(("arbitrary",) * 3, 48),
        name="dil_attention",
    )(slopes, p3, p3, p3, p3, p3, p3, p3, *tables)


def _dil_outproj_kernel(o_ref, w_ref, x_ref, y_ref):
    y_ref[0] = x_ref[0] + jnp.dot(o_ref[...], w_ref[...], preferred_element_type=f32)


def dil_outproj_residual(o3, w, x2d):
    xspec = pl.BlockSpec((1, DIL_ROWS, D_MODEL), lambda t, r: (t, 0, r))
    out = pl.pallas_call(
        _dil_outproj_kernel,
        out_shape=jax.ShapeDtypeStruct((TOKENS // DIL_TILE, DIL_ROWS, DIL_RES * D_MODEL), f32),
        grid=(TOKENS // DIL_TILE, DIL_RES),
        in_specs=[pl.BlockSpec((DIL_ROWS, D_MODEL), lambda t, r: (t * DIL_RES + r, 0)),
                  pl.BlockSpec((D_MODEL, D_MODEL), lambda t, r: (0, 0)),
                  xspec],
        out_specs=xspec,
        compiler_params=_params(("arbitrary", "arbitrary"), 48),
        name="dil_outproj_residual",
    )(o3.reshape(TOKENS, D_MODEL), w, _residue_view(x2d))
    return out.reshape(TOKENS, D_MODEL)


def _fox_layer(x2d, w_in, b_f, q_gain, k_gain, w_out, norm_g):
    h2d = rmsnorm(x2d, norm_g)
    h3 = h2d.reshape(BATCH, SEQ, D_MODEL)
    wb = w_in.astype(bf16)
    qkv_cols = 3 * FOX_HEADS * HEAD_DIM
    wf = jnp.pad(wb[:, qkv_cols:], ((0, 0), (0, HEAD_DIM - FOX_HEADS)))
    bfp = jnp.pad(b_f.astype(f32), (0, HEAD_DIM - FOX_HEADS)).reshape(1, HEAD_DIM)
    c_hi, c_mid, c_lo = (c[:, :, :FOX_HEADS] for c in fox_gate(h3, wf, bfp))

    one = jnp.ones_like(c_hi)
    ka = jnp.stack([one, one, one, -c_hi, -c_mid, -c_lo], axis=-1)
    ka = jnp.pad(jnp.transpose(ka, (0, 2, 1, 3)), ((0, 0),) * 3 + ((0, HEAD_DIM - 6),))
    qa = jnp.stack([c_hi, c_mid, c_lo, one, one, one], axis=1)
    qaT = jnp.pad(jnp.transpose(qa, (0, 3, 1, 2)), ((0, 0), (0, 0), (0, HEAD_DIM - 6), (0, 0)))

    nb = D_MODEL // PROJ_BN
    qg = (q_gain * (QK_SCALE * LOG2E)).astype(f32).reshape(1, HEAD_DIM)
    kg = k_gain.astype(f32).reshape(1, HEAD_DIM)
    qT = fox_proj(h3, wb, qg, "q", 0)
    k = fox_proj(h3, wb, kg, "k", nb)
    vT = fox_proj(h3, wb, kg, "v", 2 * nb)
    o = fox_attention(qT, qaT, k, ka, vT)
    return outproj_residual(o.reshape(TOKENS, D_MODEL), w_out.astype(bf16), x2d)


def _dil_layer(x2d, w_in, q_gain, k_gain, w_out, norm_g):
    h2d = rmsnorm(x2d, norm_g)
    ones = jnp.ones((DIL_COLS // PROJ_BN - DIL_QK_BLOCKS, HEAD_DIM), f32)
    gains = jnp.concatenate([q_gain.astype(f32) * (QK_SCALE * LOG2E), k_gain.astype(f32), ones], axis=0)
    p3 = dil_proj(h2d, w_in.astype(bf16), gains.reshape(-1, 1, HEAD_DIM))
    n_heads = N_GROUPS * DIL_HEADS
    slopes = jnp.exp2(-ALIBI_MAX_EXP * jnp.arange(1, n_heads + 1, dtype=f32) / n_heads)
    o3 = dil_attention(p3, slopes)
    return dil_outproj_residual(o3, w_out.astype(bf16), x2d)


def kernel(x, fox_w_in, fox_b_f, fox_q_gain, fox_k_gain, fox_w_out, dil_w_in, dil_q_gain, dil_k_gain,
           dil_w_out, mix_norm_g, mlp_norm_g, mlp_w_up, mlp_w_down):
    x2d = x.reshape(TOKENS, D_MODEL)
    x2d = _fox_layer(x2d, fox_w_in[0], fox_b_f[0], fox_q_gain[0], fox_k_gain[0], fox_w_out[0], mix_norm_g[0])
    x2d = mlp_residual(rmsnorm(x2d, mlp_norm_g[0]), mlp_w_up[0].astype(bf16), mlp_w_down[0].astype(bf16), x2d)
    x2d = _dil_layer(x2d, dil_w_in[0], dil_q_gain[0], dil_k_gain[0], dil_w_out[0], mix_norm_g[1])
    x2d = mlp_residual(rmsnorm(x2d, mlp_norm_g[1]), mlp_w_up[1].astype(bf16), mlp_w_down[1].astype(bf16), x2d)
    return x2d.reshape(BATCH, SEQ, D_MODEL)
```

```python
import functools

import numpy as np
import jax
import jax.numpy as jnp
from jax import lax
from jax.experimental import pallas as pl
from jax.experimental.pallas import tpu as pltpu

f32 = jnp.float32
bf16 = jnp.bfloat16

BATCH = 2
SEQ = 8192
D_MODEL = 2048
TOKENS = BATCH * SEQ
HEAD_DIM = 128
EPS = 1e-6
D_FF = 4 * D_MODEL
FOX_HEADS = D_MODEL // HEAD_DIM
DIL_PATTERNS = ((128, 1), (512, 4), (2048, 16))
N_GROUPS = len(DIL_PATTERNS)
DIL_SPAN = 128
DIL_HEADS = D_MODEL // (2 * HEAD_DIM)
DIL_V_DIM = D_MODEL // DIL_HEADS
ALIBI_MAX_EXP = 8.0

LOG2E = 1.4426950408889634
QK_SCALE = HEAD_DIM ** -0.5
MASKED = 1e30
MIB = 1024 * 1024

DIL_TILE = 2048
DIL_RES = 16
DIL_ROWS = DIL_TILE // DIL_RES


def _params(semantics, vmem_mib):
    return pltpu.CompilerParams(dimension_semantics=semantics, vmem_limit_bytes=vmem_mib * MIB)


def _rmsnorm_kernel(x_ref, g_ref, o_ref):
    x = x_ref[...]
    ms = jnp.mean(x * x, axis=-1, keepdims=True)
    o_ref[...] = (x * lax.rsqrt(ms + EPS) * g_ref[...]).astype(o_ref.dtype)


def rmsnorm(x2d, g):
    bm = 512
    return pl.pallas_call(
        _rmsnorm_kernel,
        out_shape=jax.ShapeDtypeStruct((TOKENS, D_MODEL), bf16),
        grid=(TOKENS // bm,),
        in_specs=[pl.BlockSpec((bm, D_MODEL), lambda i: (i, 0)),
                  pl.BlockSpec((1, D_MODEL), lambda i: (0, 0))],
        out_specs=pl.BlockSpec((bm, D_MODEL), lambda i: (i, 0)),
        compiler_params=_params(("arbitrary",), 32),
        name="rmsnorm",
    )(x2d, g.reshape(1, D_MODEL))


def _head_rmsnorm(y, gain):
    ms = jnp.mean(y * y, axis=-1, keepdims=True)
    return y * lax.rsqrt(ms + EPS) * gain


GATE_BM = 512


def _gate_kernel(h_ref, wf_ref, bf_ref, hi_ref, mid_ref, lo_ref, carry_ref):
    @pl.when(pl.program_id(1) == 0)
    def _():
        carry_ref[...] = jnp.zeros_like(carry_ref)

    f = jnp.dot(h_ref[0], wf_ref[...], preferred_element_type=f32) + bf_ref[...]
    lf = jnp.minimum(f, 0.0) - jnp.log1p(jnp.exp(-jnp.abs(f)))
    row = lax.broadcasted_iota(jnp.int32, (GATE_BM, GATE_BM), 0)
    col = lax.broadcasted_iota(jnp.int32, (GATE_BM, GATE_BM), 1)
    tri = (col <= row).astype(bf16)
    p0 = lf.astype(bf16)
    r0 = lf - p0.astype(f32)
    p1 = r0.astype(bf16)
    p2 = (r0 - p1.astype(f32)).astype(bf16)
    cs = (jnp.dot(tri, p0, preferred_element_type=f32)
          + jnp.dot(tri, p1, preferred_element_type=f32)
          + jnp.dot(tri, p2, preferred_element_type=f32))
    c = cs + carry_ref[...]
    carry_ref[...] = c[GATE_BM - 1:GATE_BM, :]
    c2 = c * LOG2E
    hi = c2.astype(bf16)
    r1 = c2 - hi.astype(f32)
    mid = r1.astype(bf16)
    lo = (r1 - mid.astype(f32)).astype(bf16)
    hi_ref[0] = hi
    mid_ref[0] = mid
    lo_ref[0] = lo


def fox_gate(h3, wf, bfp):
    piece = jax.ShapeDtypeStruct((BATCH, SEQ, HEAD_DIM), bf16)
    spec = pl.BlockSpec((1, GATE_BM, HEAD_DIM), lambda b, s: (b, s, 0))
    return pl.pallas_call(
        _gate_kernel,
        out_shape=(piece, piece, piece),
        grid=(BATCH, SEQ // GATE_BM),
        in_specs=[pl.BlockSpec((1, GATE_BM, D_MODEL), lambda b, s: (b, s, 0)),
                  pl.BlockSpec((D_MODEL, HEAD_DIM), lambda b, s: (0, 0)),
                  pl.BlockSpec((1, HEAD_DIM), lambda b, s: (0, 0))],
        out_specs=(spec, spec, spec),
        scratch_shapes=[pltpu.VMEM((1, HEAD_DIM), f32)],
        compiler_params=_params(("arbitrary", "arbitrary"), 32),
        name="fox_gate",
    )(h3, wf, bfp)


PROJ_BM = 1024
PROJ_BN = 1024
PROJ_HEADS = PROJ_BN // HEAD_DIM
MXU_N = 256
FOX_BK = 256


def _fox_proj_kernel(h_ref, w_ref, g_ref, o_ref, *, mode):
    h = h_ref[0]
    for s in range(PROJ_BN // MXU_N):
        y2 = jnp.dot(h, w_ref[:, s * MXU_N:(s + 1) * MXU_N], preferred_element_type=f32)
        for u in range(MXU_N // HEAD_DIM):
            hh = s * (MXU_N // HEAD_DIM) + u
            y = y2[:, u * HEAD_DIM:(u + 1) * HEAD_DIM]
            if mode == "q":
                o_ref[0, hh] = _head_rmsnorm(y, g_ref[...]).T.astype(bf16)
            elif mode == "k":
                o_ref[0, hh] = _head_rmsnorm(y, g_ref[...]).astype(bf16)
            else:
                for c in range(PROJ_BM // FOX_BK):
                    o_ref[0, hh, c] = y[c * FOX_BK:(c + 1) * FOX_BK, :].T.astype(bf16)


def fox_proj(h3, w, gain, mode, col_block0):
    nsb = SEQ // PROJ_BM
    if mode == "q":
        shape = (BATCH, FOX_HEADS, HEAD_DIM, SEQ)
        ospec = pl.BlockSpec((1, PROJ_HEADS, HEAD_DIM, PROJ_BM), lambda b, s, j: (b, j, 0, s))
    elif mode == "k":
        shape = (BATCH, FOX_HEADS, SEQ, HEAD_DIM)
        ospec = pl.BlockSpec((1, PROJ_HEADS, PROJ_BM, HEAD_DIM), lambda b, s, j: (b, j, s, 0))
    else:
        shape = (BATCH, FOX_HEADS, SEQ // FOX_BK, HEAD_DIM, FOX_BK)
        ospec = pl.BlockSpec((1, PROJ_HEADS, PROJ_BM // FOX_BK, HEAD_DIM, FOX_BK),
                             lambda b, s, j: (b, j, s, 0, 0))
    return pl.pallas_call(
        functools.partial(_fox_proj_kernel, mode=mode),
        out_shape=jax.ShapeDtypeStruct(shape, bf16),
        grid=(BATCH, nsb, D_MODEL // PROJ_BN),
        in_specs=[pl.BlockSpec((1, PROJ_BM, D_MODEL), lambda b, s, j: (b, s, 0)),
                  pl.BlockSpec((D_MODEL, PROJ_BN), lambda b, s, j: (0, col_block0 + j)),
                  pl.BlockSpec((1, HEAD_DIM), lambda b, s, j: (0, 0))],
        out_specs=ospec,
        compiler_params=_params(("arbitrary",) * 3, 48),
        name="fox_proj_" + mode,
    )(h3, w, gain)


FOX_BQ = 1024
FOX_TILES = FOX_BQ // FOX_BK
AUG = 2 * HEAD_DIM
AUG_ROWS = 16
AUG_CHUNK = 1024


def _fox_attn_kernel(qT_ref, chi_ref, cmid_ref, clo_ref, k_ref, vT_ref, o_ref,
                     kaug_ref, s_ref, m_ref, l_ref, acc_ref):
    h = pl.program_id(1)
    qi = pl.program_id(2)
    pieces = (chi_ref, cmid_ref, clo_ref)
    n_p = len(pieces)

    @pl.when(qi == 0)
    def _():
        kaug_ref[:, :HEAD_DIM] = k_ref[0, 0]
        row = lax.broadcasted_iota(jnp.int32, (HEAD_DIM, HEAD_DIM), 0)
        col = lax.broadcasted_iota(jnp.int32, (HEAD_DIM, HEAD_DIM), 1)
        lane = lax.broadcasted_iota(jnp.int32, (AUG_CHUNK, HEAD_DIM), 1)
        for c in range(SEQ // AUG_CHUNK):
            rows = slice(c * AUG_CHUNK, (c + 1) * AUG_CHUNK)
            aug = jnp.where(lane < n_p, 1.0, 0.0)
            for p, ref in enumerate(pieces):
                sel = jnp.where((row == h) & (col == n_p + p), -1.0, 0.0).astype(bf16)
                aug = aug + jnp.dot(ref[0, rows, :], sel, preferred_element_type=f32)
            kaug_ref[rows, HEAD_DIM:] = aug.astype(bf16)

    q0 = pl.multiple_of(qi * FOX_BQ, FOX_BQ)
    row = lax.broadcasted_iota(jnp.int32, (AUG_ROWS, HEAD_DIM), 0)
    col = lax.broadcasted_iota(jnp.int32, (AUG_ROWS, HEAD_DIM), 1)
    slot = lax.broadcasted_iota(jnp.int32, (AUG_ROWS, FOX_BQ), 0)
    qa = jnp.where((slot >= n_p) & (slot < 2 * n_p), 1.0, 0.0)
    for p, ref in enumerate(pieces):
        sel = jnp.where((row == p) & (col == h), 1.0, 0.0).astype(bf16)
        qa = qa + lax.dot_general(sel, ref[0, pl.ds(q0, FOX_BQ), :], (((1,), (1,)), ((), ())),
                                  preferred_element_type=f32)
    q_aug = jnp.concatenate([qT_ref[0, 0], qa.astype(bf16),
                             jnp.zeros((HEAD_DIM - AUG_ROWS, FOX_BQ), bf16)], axis=0)

    def scores(kb, q):
        start = pl.multiple_of(kb * FOX_BK, FOX_BK)
        return jnp.dot(kaug_ref[pl.ds(start, FOX_BK), :], q, preferred_element_type=f32)

    def softmax_step(s, m, l):
        m_new = jnp.maximum(m, jnp.max(s, axis=0, keepdims=True))
        alpha = jnp.exp2(m - m_new)
        p = jnp.exp2(s - m_new)
        return m_new, alpha * l + jnp.sum(p, axis=0, keepdims=True), alpha, p.astype(bf16)

    def pv(kb, p):
        return jnp.dot(vT_ref[0, 0, kb], p, preferred_element_type=f32)

    s_ref[...] = scores(0, q_aug)
    m_ref[...] = jnp.full_like(m_ref, -MASKED)
    l_ref[...] = jnp.zeros_like(l_ref)
    acc_ref[...] = jnp.zeros_like(acc_ref)

    def body(j, carry):
        kb0 = j * FOX_TILES
        s, m, l, acc = s_ref[...], m_ref[...], l_ref[...], acc_ref[...]
        for u in range(FOX_TILES):
            s_next = scores(kb0 + u + 1, q_aug)
            m, l, alpha, p = softmax_step(s, m, l)
            acc = alpha * acc + pv(kb0 + u, p)
            s = s_next
        s_ref[...], m_ref[...], l_ref[...], acc_ref[...] = s, m, l, acc
        return carry

    lax.fori_loop(0, qi, body, 0)

    kb0 = qi * FOX_TILES
    kk = lax.broadcasted_iota(jnp.int32, (FOX_BK, FOX_BK), 0)
    qq = lax.broadcasted_iota(jnp.int32, (FOX_BK, FOX_BK), 1)
    s = s_ref[...]
    for d in range(FOX_TILES):
        lo = d * FOX_BK
        if d + 1 < FOX_TILES:
            s_next = scores(kb0 + d + 1, q_aug[:, lo + FOX_BK:])
        edge = jnp.where(kk <= qq, s[:, :FOX_BK], -MASKED)
        s = edge if d + 1 == FOX_TILES else jnp.concatenate([edge, s[:, FOX_BK:]], axis=1)
        m, l, alpha, p = softmax_step(s, m_ref[:, lo:], l_ref[:, lo:])
        acc_ref[:, lo:] = alpha * acc_ref[:, lo:] + pv(kb0 + d, p)
        m_ref[:, lo:] = m
        l_ref[:, lo:] = l
        if d + 1 < FOX_TILES:
            s = s_next
    o = acc_ref[...] / l_ref[...]
    o_ref[0] = o.T.astype(bf16)


def fox_attention(qT, c_pieces, k, vT):
    nq = SEQ // FOX_BQ
    nk = SEQ // FOX_BK
    cspec = pl.BlockSpec((1, SEQ, HEAD_DIM), lambda b, h, i: (b, 0, 0))
    return pl.pallas_call(
        _fox_attn_kernel,
        out_shape=jax.ShapeDtypeStruct((BATCH, SEQ, D_MODEL), bf16),
        grid=(BATCH, FOX_HEADS, nq),
        in_specs=[pl.BlockSpec((1, 1, HEAD_DIM, FOX_BQ), lambda b, h, i: (b, h, 0, i)),
                  cspec, cspec, cspec,
                  pl.BlockSpec((1, 1, SEQ, HEAD_DIM), lambda b, h, i: (b, h, 0, 0)),
                  pl.BlockSpec((1, 1, nk, HEAD_DIM, FOX_BK), lambda b, h, i: (b, h, 0, 0, 0))],
        out_specs=pl.BlockSpec((1, FOX_BQ, HEAD_DIM), lambda b, h, i: (b, i, h)),
        scratch_shapes=[pltpu.VMEM((SEQ, AUG), bf16),
                        pltpu.VMEM((FOX_BK, FOX_BQ), f32),
                        pltpu.VMEM((1, FOX_BQ), f32),
                        pltpu.VMEM((1, FOX_BQ), f32),
                        pltpu.VMEM((HEAD_DIM, FOX_BQ), f32)],
        compiler_params=_params(("arbitrary",) * 3, 48),
        name="fox_attention",
    )(qT, *c_pieces, k, vT)


OUT_BM = 512
OUT_BN = 1024


def _outproj_kernel(o_ref, w_ref, x_ref, y_ref):
    y_ref[...] = x_ref[...] + jnp.dot(o_ref[...], w_ref[...], preferred_element_type=f32)


def outproj_residual(o2d, w, x2d):
    return pl.pallas_call(
        _outproj_kernel,
        out_shape=jax.ShapeDtypeStruct((TOKENS, D_MODEL), f32),
        grid=(TOKENS // OUT_BM, D_MODEL // OUT_BN),
        in_specs=[pl.BlockSpec((OUT_BM, D_MODEL), lambda i, j: (i, 0)),
                  pl.BlockSpec((D_MODEL, OUT_BN), lambda i, j: (0, j)),
                  pl.BlockSpec((OUT_BM, OUT_BN), lambda i, j: (i, j))],
        out_specs=pl.BlockSpec((OUT_BM, OUT_BN), lambda i, j: (i, j)),
        compiler_params=_params(("arbitrary", "arbitrary"), 40),
        name="outproj_residual",
    )(o2d, w, x2d)


MLP_BM = 512
MLP_BF = 512


def _mlp_kernel(h_ref, wu_ref, wd_ref, x_ref, y_ref, acc_ref):
    f = pl.program_id(1)

    @pl.when(f == 0)
    def _():
        acc_ref[...] = jnp.zeros_like(acc_ref)

    a = jnp.maximum(jnp.dot(h_ref[...], wu_ref[...], preferred_element_type=f32), 0.0)
    acc_ref[...] += jnp.dot((a * a).astype(bf16), wd_ref[...], preferred_element_type=f32)

    @pl.when(f == pl.num_programs(1) - 1)
    def _():
        y_ref[...] = x_ref[...] + acc_ref[...]


def mlp_residual(h2d, w_up, w_down, x2d):
    return pl.pallas_call(
        _mlp_kernel,
        out_shape=jax.ShapeDtypeStruct((TOKENS, D_MODEL), f32),
        grid=(TOKENS // MLP_BM, D_FF // MLP_BF),
        in_specs=[pl.BlockSpec((MLP_BM, D_MODEL), lambda i, f: (i, 0)),
                  pl.BlockSpec((D_MODEL, MLP_BF), lambda i, f: (0, f)),
                  pl.BlockSpec((MLP_BF, D_MODEL), lambda i, f: (f, 0)),
                  pl.BlockSpec((MLP_BM, D_MODEL), lambda i, f: (i, 0))],
        out_specs=pl.BlockSpec((MLP_BM, D_MODEL), lambda i, f: (i, 0)),
        scratch_shapes=[pltpu.VMEM((MLP_BM, D_MODEL), f32)],
        compiler_params=_params(("arbitrary", "arbitrary"), 48),
        name="mlp_residual",
    )(h2d, w_up, w_down, x2d)


DIL_COLS = 2 * N_GROUPS * DIL_HEADS * HEAD_DIM + DIL_HEADS * DIL_V_DIM
DIL_HALF = PROJ_BM // DIL_RES
DIL_QK_BLOCKS = 2 * N_GROUPS
LANES = 128


def _matmul_to_slabs(a, w_ref, slab_ref):
    per = MXU_N // LANES
    for s in range(w_ref.shape[1] // MXU_N):
        y = jnp.dot(a, w_ref[:, s * MXU_N:(s + 1) * MXU_N], preferred_element_type=f32)
        for u in range(per):
            slab_ref[s * per + u] = y[:, u * LANES:(u + 1) * LANES]


def _dil_proj_kernel(h_ref, w_ref, g_ref, o_ref, slab_ref):
    j = pl.program_id(1)
    _matmul_to_slabs(h_ref[...], w_ref, slab_ref)

    @pl.when(j < DIL_QK_BLOCKS)
    def _():
        for rho in range(DIL_RES):
            for hh in range(PROJ_HEADS):
                y = slab_ref[hh, pl.ds(rho, DIL_HALF, stride=DIL_RES), :]
                o_ref[0, rho, :, hh * HEAD_DIM:(hh + 1) * HEAD_DIM] = _head_rmsnorm(y, g_ref[0]).astype(bf16)

    @pl.when(j >= DIL_QK_BLOCKS)
    def _():
        for rho in range(DIL_RES):
            for c in range(PROJ_BN // LANES):
                o_ref[0, rho, :, c * LANES:(c + 1) * LANES] = (
                    slab_ref[c, pl.ds(rho, DIL_HALF, stride=DIL_RES), :].astype(bf16))


def dil_proj(h2d, w, gains):
    halves = DIL_TILE // PROJ_BM
    out = pl.pallas_call(
        _dil_proj_kernel,
        out_shape=jax.ShapeDtypeStruct((TOKENS // DIL_TILE, DIL_RES, DIL_ROWS, DIL_COLS), bf16),
        grid=(TOKENS // PROJ_BM, DIL_COLS // PROJ_BN),
        in_specs=[pl.BlockSpec((PROJ_BM, D_MODEL), lambda i, j: (i, 0)),
                  pl.BlockSpec((D_MODEL, PROJ_BN), lambda i, j: (0, j)),
                  pl.BlockSpec((1, 1, HEAD_DIM), lambda i, j: (j, 0, 0))],
        out_specs=pl.BlockSpec((1, DIL_RES, DIL_HALF, PROJ_BN),
                               lambda i, j: (i // halves, 0, i % halves, j)),
        scratch_shapes=[pltpu.VMEM((PROJ_BN // LANES, PROJ_BM, LANES), f32)],
        compiler_params=_params(("arbitrary", "arbitrary"), 48),
        name="dil_proj",
    )(h2d, w, gains)
    return out.reshape(BATCH, SEQ, DIL_COLS)


def _dil_distance_tables():
    def table(delta, prev_half, r):
        valid = (delta >= 0) & (delta <= DIL_SPAN)
        d = np.where(valid, delta * (r * LOG2E), MASKED).astype(np.float32)
        first = np.where(prev_half, np.float32(MASKED), d).astype(np.float32)
        return d, first

    i = np.arange(128)[:, None]
    j = np.arange(256)[None, :]
    t16 = table(128 + i - j, j < 128, 16)
    c, u = i // 32, i % 32
    cp, half, up = j // 64, (j % 64) // 32, j % 32
    t4 = table(128 + 4 * u + c - 128 * half - 4 * up - cp, half == 0, 4)
    i = np.arange(256)[:, None]
    j = np.arange(512)[None, :]
    rho, u = i // 16, i % 16
    rp, half, up = j // 32, (j % 32) // 16, j % 16
    t1 = table(256 + 16 * u + rho - 256 * half - 16 * up - rp, half == 0, 1)
    return t1, t4, t16


def _dil_attn_kernel(slopes_ref, q0_ref, q1_ref, q2_ref, k0_ref, k1_ref, k2_ref, v_ref,
                     d1_ref, d1f_ref, d4_ref, d4f_ref, d16_ref, d16f_ref,
                     o_ref, m_ref, l_ref, acc_ref):
    h = pl.program_id(1)
    t = pl.program_id(2)
    base = pl.multiple_of(t * DIL_TILE, DIL_TILE)
    pbase = pl.multiple_of(jnp.maximum(t - 1, 0) * DIL_TILE, DIL_TILE)
    first = t == 0

    def attend(q, kw, vw, bias, rows, size, init):
        s = lax.dot_general(q, kw, (((1,), (1,)), ((), ())), preferred_element_type=f32) + bias
        m_blk = jnp.max(s, axis=1, keepdims=True)
        if init:
            m_new = m_blk
        else:
            m_old = jnp.concatenate([m_ref[r:r + size, :] for r in rows], axis=0)
            l_old = jnp.concatenate([l_ref[r:r + size, :] for r in rows], axis=0)
            o_old = jnp.concatenate([acc_ref[r:r + size, :] for r in rows], axis=0)
            m_new = jnp.maximum(m_old, m_blk)
        p = jnp.exp2(s - m_new)
        l_new = jnp.sum(p, axis=1, keepdims=True)
        o_new = jnp.dot(p.astype(bf16), vw, preferred_element_type=f32)
        if not init:
            alpha = jnp.exp2(m_old - m_new)
            l_new = alpha * l_old + l_new
            o_new = alpha * o_old + o_new
        for c, r in enumerate(rows):
            m_ref[r:r + size, :] = m_new[c * size:(c + 1) * size]
            l_ref[r:r + size, :] = l_new[c * size:(c + 1) * size]
            acc_ref[r:r + size, :] = o_new[c * size:(c + 1) * size]

    def window(ref, rows, size, n):
        if n > 0:
            return jnp.concatenate([ref[0, pl.ds(base + (r - size), 2 * size), :] for r in rows], axis=0)
        parts = []
        for r in rows:
            parts.append(ref[0, pl.ds(pbase + (r + DIL_ROWS - size), size), :])
            parts.append(ref[0, pl.ds(base + r, size), :])
        return jnp.concatenate(parts, axis=0)

    slope = slopes_ref[2 * DIL_HEADS + h]
    bias = -slope * jnp.where(first, d16f_ref[...], d16_ref[...])
    for rho in range(DIL_RES):
        rows = [rho * DIL_ROWS]
        q = q2_ref[0, rows[0]:rows[0] + DIL_ROWS, :]
        attend(q, window(k2_ref, rows, DIL_ROWS, 0), window(v_ref, rows, DIL_ROWS, 0),
               bias, rows, DIL_ROWS, True)

    slope = slopes_ref[DIL_HEADS + h]
    bias_in = -slope * d4_ref[...]
    bias_edge = -slope * jnp.where(first, d4f_ref[...], d4_ref[...])
    for rho4 in range(4):
        for n in range(4):
            rows = [(rho4 + 4 * c) * DIL_ROWS + 32 * n for c in range(4)]
            q = jnp.concatenate([q1_ref[0, r:r + 32, :] for r in rows], axis=0)
            attend(q, window(k1_ref, rows, 32, n), window(v_ref, rows, 32, n),
                   bias_edge if n == 0 else bias_in, rows, 32, False)

    slope = slopes_ref[h]
    bias_in = -slope * d1_ref[...]
    bias_edge = -slope * jnp.where(first, d1f_ref[...], d1_ref[...])
    for n in range(DIL_ROWS // 16):
        rows = [rho * DIL_ROWS + 16 * n for rho in range(DIL_RES)]
        q = jnp.concatenate([q0_ref[0, r:r + 16, :] for r in rows], axis=0)
        attend(q, window(k0_ref, rows, 16, n), window(v_ref, rows, 16, n),
               bias_edge if n == 0 else bias_in, rows, 16, False)

    o_ref[0] = (acc_ref[...] / l_ref[...]).astype(bf16)


def dil_attention(p3, slopes):
    tables = [jnp.asarray(a) for pair in _dil_distance_tables() for a in pair]
    gh = DIL_HEADS
    kcol0 = N_GROUPS * gh
    vcol0 = 2 * N_GROUPS * gh * HEAD_DIM // DIL_V_DIM
    qspecs = [pl.BlockSpec((1, DIL_TILE, HEAD_DIM), lambda b, h, t, g=g: (b, t, g * gh + h))
              for g in range(N_GROUPS)]
    kspecs = [pl.BlockSpec((1, SEQ, HEAD_DIM), lambda b, h, t, g=g: (b, 0, kcol0 + g * gh + h))
              for g in range(N_GROUPS)]
    vspec = pl.BlockSpec((1, SEQ, DIL_V_DIM), lambda b, h, t: (b, 0, vcol0 + h))
    tspecs = [pl.BlockSpec(a.shape, lambda b, h, t: (0, 0)) for a in tables]
    return pl.pallas_call(
        _dil_attn_kernel,
        out_shape=jax.ShapeDtypeStruct((BATCH, SEQ, D_MODEL), bf16),
        grid=(BATCH, DIL_HEADS, SEQ // DIL_TILE),
        in_specs=[pl.BlockSpec(memory_space=pltpu.SMEM)] + qspecs + kspecs + [vspec] + tspecs,
        out_specs=pl.BlockSpec((1, DIL_TILE, DIL_V_DIM), lambda b, h, t: (b, t, h)),
        scratch_shapes=[pltpu.VMEM((DIL_TILE, 1), f32),
                        pltpu.VMEM((DIL_TILE, 1), f32),
                        pltpu.VMEM((DIL_TILE, DIL_V_DIM), f32)],
        compiler_params=_params(("arbitrary",) * 3, 48),
        name="dil_attention",
    )(slopes, p3, p3, p3, p3, p3, p3, p3, *tables)


def _dil_outproj_kernel(o_ref, w_ref, x_ref, y_ref, nat_ref):
    o = o_ref[0].reshape(PROJ_BM, D_MODEL)
    per = MXU_N // LANES
    for s in range(OUT_BN // MXU_N):
        mix = jnp.dot(o, w_ref[:, s * MXU_N:(s + 1) * MXU_N], preferred_element_type=f32)
        for u in range(per):
            c = s * per + u
            for rho in range(DIL_RES):
                nat_ref[c, pl.ds(rho, DIL_HALF, stride=DIL_RES), :] = (
                    mix[rho * DIL_HALF:(rho + 1) * DIL_HALF, u * LANES:(u + 1) * LANES])
            cols = slice(c * LANES, (c + 1) * LANES)
            y_ref[:, cols] = x_ref[:, cols] + nat_ref[c]


def dil_outproj_residual(o3, w, x2d):
    halves = DIL_TILE // PROJ_BM
    o4 = o3.reshape(TOKENS // DIL_TILE, DIL_RES, DIL_ROWS, D_MODEL)
    slabs = pltpu.VMEM((OUT_BN // LANES, PROJ_BM, LANES), f32)
    return pl.pallas_call(
        _dil_outproj_kernel,
        out_shape=jax.ShapeDtypeStruct((TOKENS, D_MODEL), f32),
        grid=(TOKENS // PROJ_BM, D_MODEL // OUT_BN),
        in_specs=[pl.BlockSpec((1, DIL_RES, DIL_HALF, D_MODEL), lambda i, j: (i // halves, 0, i % halves, 0)),
                  pl.BlockSpec((D_MODEL, OUT_BN), lambda i, j: (0, j)),
                  pl.BlockSpec((PROJ_BM, OUT_BN), lambda i, j: (i, j))],
        out_specs=pl.BlockSpec((PROJ_BM, OUT_BN), lambda i, j: (i, j)),
        scratch_shapes=[slabs],
        compiler_params=_params(("arbitrary", "arbitrary"), 48),
        name="dil_outproj_residual",
    )(o4, w, x2d)


def _fox_layer(x2d, w_in, b_f, q_gain, k_gain, w_out, norm_g):
    h2d = rmsnorm(x2d, norm_g)
    h3 = h2d.reshape(BATCH, SEQ, D_MODEL)
    wb = w_in.astype(bf16)
    qkv_cols = 3 * FOX_HEADS * HEAD_DIM
    wf = jnp.pad(wb[:, qkv_cols:], ((0, 0), (0, HEAD_DIM - FOX_HEADS)))
    bfp = jnp.pad(b_f.astype(f32), (0, HEAD_DIM - FOX_HEADS)).reshape(1, HEAD_DIM)
    c_pieces = fox_gate(h3, wf, bfp)

    nb = D_MODEL // PROJ_BN
    qg = (q_gain * (QK_SCALE * LOG2E)).astype(f32).reshape(1, HEAD_DIM)
    kg = k_gain.astype(f32).reshape(1, HEAD_DIM)
    qT = fox_proj(h3, wb, qg, "q", 0)
    k = fox_proj(h3, wb, kg, "k", nb)
    vT = fox_proj(h3, wb, kg, "v", 2 * nb)
    o = fox_attention(qT, c_pieces, k, vT)
    return outproj_residual(o.reshape(TOKENS, D_MODEL), w_out.astype(bf16), x2d)


def _dil_layer(x2d, w_in, q_gain, k_gain, w_out, norm_g):
    h2d = rmsnorm(x2d, norm_g)
    ones = jnp.ones((DIL_COLS // PROJ_BN - DIL_QK_BLOCKS, HEAD_DIM), f32)
    gains = jnp.concatenate([q_gain.astype(f32) * (QK_SCALE * LOG2E), k_gain.astype(f32), ones], axis=0)
    p3 = dil_proj(h2d, w_in.astype(bf16), gains.reshape(-1, 1, HEAD_DIM))
    n_heads = N_GROUPS * DIL_HEADS
    slopes = jnp.exp2(-ALIBI_MAX_EXP * jnp.arange(1, n_heads + 1, dtype=f32) / n_heads)
    o3 = dil_attention(p3, slopes)
    return dil_outproj_residual(o3, w_out.astype(bf16), x2d)


def kernel(x, fox_w_in, fox_b_f, fox_q_gain, fox_k_gain, fox_w_out, dil_w_in, dil_q_gain, dil_k_gain,
           dil_w_out, mix_norm_g, mlp_norm_g, mlp_w_up, mlp_w_down):
    x2d = x.reshape(TOKENS, D_MODEL)
    x2d = _fox_layer(x2d, fox_w_in[0], fox_b_f[0], fox_q_gain[0], fox_k_gain[0], fox_w_out[0], mix_norm_g[0])
    x2d = mlp_residual(rmsnorm(x2d, mlp_norm_g[0]), mlp_w_up[0].astype(bf16), mlp_w_down[0].astype(bf16), x2d)
    x2d = _dil_layer(x2d, dil_w_in[0], dil_q_gain[0], dil_k_gain[0], dil_w_out[0], mix_norm_g[1])
    x2d = mlp_residual(rmsnorm(x2d, mlp_norm_g[1]), mlp_w_up[1].astype(bf16), mlp_w_down[1].astype(bf16), x2d)
    return x2d.reshape(BATCH, SEQ, D_MODEL)
```

```python
import functools

import numpy as np
import jax
import jax.numpy as jnp
from jax import lax
from jax.experimental import pallas as pl
from jax.experimental.pallas import tpu as pltpu

f32 = jnp.float32
bf16 = jnp.bfloat16

BATCH = 2
SEQ = 8192
D_MODEL = 2048
TOKENS = BATCH * SEQ
HEAD_DIM = 128
EPS = 1e-6
D_FF = 4 * D_MODEL
FOX_HEADS = D_MODEL // HEAD_DIM
DIL_PATTERNS = ((128, 1), (512, 4), (2048, 16))
N_GROUPS = len(DIL_PATTERNS)
DIL_SPAN = 128
DIL_HEADS = D_MODEL // (2 * HEAD_DIM)
DIL_V_DIM = D_MODEL // DIL_HEADS
ALIBI_MAX_EXP = 8.0

LOG2E = 1.4426950408889634
QK_SCALE = HEAD_DIM ** -0.5
MASKED = 1e30
MIB = 1024 * 1024

DIL_TILE = 2048
DIL_RES = 16
DIL_ROWS = DIL_TILE // DIL_RES


def _params(semantics, vmem_mib):
    return pltpu.CompilerParams(dimension_semantics=semantics, vmem_limit_bytes=vmem_mib * MIB)


def _rmsnorm_kernel(x_ref, g_ref, o_ref):
    x = x_ref[...]
    ms = jnp.mean(x * x, axis=-1, keepdims=True)
    o_ref[...] = (x * lax.rsqrt(ms + EPS) * g_ref[...]).astype(o_ref.dtype)


def rmsnorm(x2d, g):
    bm = 512
    return pl.pallas_call(
        _rmsnorm_kernel,
        out_shape=jax.ShapeDtypeStruct((TOKENS, D_MODEL), bf16),
        grid=(TOKENS // bm,),
        in_specs=[pl.BlockSpec((bm, D_MODEL), lambda i: (i, 0)),
                  pl.BlockSpec((1, D_MODEL), lambda i: (0, 0))],
        out_specs=pl.BlockSpec((bm, D_MODEL), lambda i: (i, 0)),
        compiler_params=_params(("arbitrary",), 32),
        name="rmsnorm",
    )(x2d, g.reshape(1, D_MODEL))


def _head_rmsnorm(y, gain):
    ms = jnp.mean(y * y, axis=-1, keepdims=True)
    return y * lax.rsqrt(ms + EPS) * gain


GATE_BM = 512


def _gate_kernel(h_ref, wf_ref, bf_ref, hi_ref, mid_ref, lo_ref, carry_ref):
    @pl.when(pl.program_id(1) == 0)
    def _():
        carry_ref[...] = jnp.zeros_like(carry_ref)

    f = jnp.dot(h_ref[0], wf_ref[...], preferred_element_type=f32) + bf_ref[...]
    lf = jnp.minimum(f, 0.0) - jnp.log1p(jnp.exp(-jnp.abs(f)))
    row = lax.broadcasted_iota(jnp.int32, (GATE_BM, GATE_BM), 0)
    col = lax.broadcasted_iota(jnp.int32, (GATE_BM, GATE_BM), 1)
    tri = (col <= row).astype(bf16)
    p0 = lf.astype(bf16)
    r0 = lf - p0.astype(f32)
    p1 = r0.astype(bf16)
    p2 = (r0 - p1.astype(f32)).astype(bf16)
    cs = (jnp.dot(tri, p0, preferred_element_type=f32)
          + jnp.dot(tri, p1, preferred_element_type=f32)
          + jnp.dot(tri, p2, preferred_element_type=f32))
    c = cs + carry_ref[...]
    carry_ref[...] = c[GATE_BM - 1:GATE_BM, :]
    c2 = c * LOG2E
    hi = c2.astype(bf16)
    r1 = c2 - hi.astype(f32)
    mid = r1.astype(bf16)
    lo = (r1 - mid.astype(f32)).astype(bf16)
    hi_ref[0] = hi
    mid_ref[0] = mid
    lo_ref[0] = lo


def fox_gate(h3, wf, bfp):
    piece = jax.ShapeDtypeStruct((BATCH, SEQ, HEAD_DIM), bf16)
    spec = pl.BlockSpec((1, GATE_BM, HEAD_DIM), lambda b, s: (b, s, 0))
    return pl.pallas_call(
        _gate_kernel,
        out_shape=(piece, piece, piece),
        grid=(BATCH, SEQ // GATE_BM),
        in_specs=[pl.BlockSpec((1, GATE_BM, D_MODEL), lambda b, s: (b, s, 0)),
                  pl.BlockSpec((D_MODEL, HEAD_DIM), lambda b, s: (0, 0)),
                  pl.BlockSpec((1, HEAD_DIM), lambda b, s: (0, 0))],
        out_specs=(spec, spec, spec),
        scratch_shapes=[pltpu.VMEM((1, HEAD_DIM), f32)],
        compiler_params=_params(("arbitrary", "arbitrary"), 32),
        name="fox_gate",
    )(h3, wf, bfp)


PROJ_BM = 1024
PROJ_BN = 1024
PROJ_HEADS = PROJ_BN // HEAD_DIM
MXU_N = 256
FOX_BK = 256


def _fox_proj_kernel(h_ref, w_ref, g_ref, o_ref, *, mode):
    h = h_ref[0]
    for s in range(PROJ_BN // MXU_N):
        y2 = jnp.dot(h, w_ref[:, s * MXU_N:(s + 1) * MXU_N], preferred_element_type=f32)
        for u in range(MXU_N // HEAD_DIM):
            hh = s * (MXU_N // HEAD_DIM) + u
            y = y2[:, u * HEAD_DIM:(u + 1) * HEAD_DIM]
            if mode == "q":
                o_ref[0, hh] = _head_rmsnorm(y, g_ref[...]).T.astype(bf16)
            elif mode == "k":
                o_ref[0, hh] = _head_rmsnorm(y, g_ref[...]).astype(bf16)
            else:
                for c in range(PROJ_BM // FOX_BK):
                    o_ref[0, hh, c] = y[c * FOX_BK:(c + 1) * FOX_BK, :].T.astype(bf16)


def fox_proj(h3, w, gain, mode, col_block0):
    nsb = SEQ // PROJ_BM
    if mode == "q":
        shape = (BATCH, FOX_HEADS, HEAD_DIM, SEQ)
        ospec = pl.BlockSpec((1, PROJ_HEADS, HEAD_DIM, PROJ_BM), lambda b, s, j: (b, j, 0, s))
    elif mode == "k":
        shape = (BATCH, FOX_HEADS, SEQ, HEAD_DIM)
        ospec = pl.BlockSpec((1, PROJ_HEADS, PROJ_BM, HEAD_DIM), lambda b, s, j: (b, j, s, 0))
    else:
        shape = (BATCH, FOX_HEADS, SEQ // FOX_BK, HEAD_DIM, FOX_BK)
        ospec = pl.BlockSpec((1, PROJ_HEADS, PROJ_BM // FOX_BK, HEAD_DIM, FOX_BK),
                             lambda b, s, j: (b, j, s, 0, 0))
    return pl.pallas_call(
        functools.partial(_fox_proj_kernel, mode=mode),
        out_shape=jax.ShapeDtypeStruct(shape, bf16),
        grid=(BATCH, nsb, D_MODEL // PROJ_BN),
        in_specs=[pl.BlockSpec((1, PROJ_BM, D_MODEL), lambda b, s, j: (b, s, 0)),
                  pl.BlockSpec((D_MODEL, PROJ_BN), lambda b, s, j: (0, col_block0 + j)),
                  pl.BlockSpec((1, HEAD_DIM), lambda b, s, j: (0, 0))],
        out_specs=ospec,
        compiler_params=_params(("arbitrary",) * 3, 48),
        name="fox_proj_" + mode,
    )(h3, w, gain)


FOX_BQ = 1024
FOX_TILES = FOX_BQ // FOX_BK
AUG = 2 * HEAD_DIM
AUG_ROWS = 16
AUG_CHUNK = 1024


def _fox_attn_kernel(qT_ref, chi_ref, cmid_ref, clo_ref, k_ref, vT_ref, o_ref,
                     kaug_ref, s_ref, m_ref, l_ref, acc_ref):
    h = pl.program_id(1)
    qi = pl.program_id(2)
    pieces = (chi_ref, cmid_ref, clo_ref)
    n_p = len(pieces)

    @pl.when(qi == 0)
    def _():
        kaug_ref[:, :HEAD_DIM] = k_ref[0, 0]
        row = lax.broadcasted_iota(jnp.int32, (HEAD_DIM, HEAD_DIM), 0)
        col = lax.broadcasted_iota(jnp.int32, (HEAD_DIM, HEAD_DIM), 1)
        lane = lax.broadcasted_iota(jnp.int32, (AUG_CHUNK, HEAD_DIM), 1)
        for c in range(SEQ // AUG_CHUNK):
            rows = slice(c * AUG_CHUNK, (c + 1) * AUG_CHUNK)
            aug = jnp.where(lane < n_p, 1.0, 0.0)
            for p, ref in enumerate(pieces):
                sel = jnp.where((row == h) & (col == n_p + p), -1.0, 0.0).astype(bf16)
                aug = aug + jnp.dot(ref[0, rows, :], sel, preferred_element_type=f32)
            kaug_ref[rows, HEAD_DIM:] = aug.astype(bf16)

    q0 = pl.multiple_of(qi * FOX_BQ, FOX_BQ)
    row = lax.broadcasted_iota(jnp.int32, (AUG_ROWS, HEAD_DIM), 0)
    col = lax.broadcasted_iota(jnp.int32, (AUG_ROWS, HEAD_DIM), 1)
    slot = lax.broadcasted_iota(jnp.int32, (AUG_ROWS, FOX_BQ), 0)
    qa = jnp.where((slot >= n_p) & (slot < 2 * n_p), 1.0, 0.0)
    for p, ref in enumerate(pieces):
        sel = jnp.where((row == p) & (col == h), 1.0, 0.0).astype(bf16)
        qa = qa + lax.dot_general(sel, ref[0, pl.ds(q0, FOX_BQ), :], (((1,), (1,)), ((), ())),
                                  preferred_element_type=f32)
    q_aug = jnp.concatenate([qT_ref[0, 0], qa.astype(bf16),
                             jnp.zeros((HEAD_DIM - AUG_ROWS, FOX_BQ), bf16)], axis=0)

    def scores(kb, q):
        start = pl.multiple_of(kb * FOX_BK, FOX_BK)
        return jnp.dot(kaug_ref[pl.ds(start, FOX_BK), :], q, preferred_element_type=f32)

    def softmax_step(s, m, l):
        m_new = jnp.maximum(m, jnp.max(s, axis=0, keepdims=True))
        alpha = jnp.exp2(m - m_new)
        p = jnp.exp2(s - m_new)
        return m_new, alpha * l + jnp.sum(p, axis=0, keepdims=True), alpha, p.astype(bf16)

    def pv(kb, p):
        return jnp.dot(vT_ref[0, 0, kb], p, preferred_element_type=f32)

    s_ref[...] = scores(0, q_aug)
    m_ref[...] = jnp.full_like(m_ref, -MASKED)
    l_ref[...] = jnp.zeros_like(l_ref)
    acc_ref[...] = jnp.zeros_like(acc_ref)

    def body(j, carry):
        kb0 = j * FOX_TILES
        s, m, l, acc = s_ref[...], m_ref[...], l_ref[...], acc_ref[...]
        for u in range(FOX_TILES):
            s_next = scores(kb0 + u + 1, q_aug)
            m, l, alpha, p = softmax_step(s, m, l)
            acc = alpha * acc + pv(kb0 + u, p)
            s = s_next
        s_ref[...], m_ref[...], l_ref[...], acc_ref[...] = s, m, l, acc
        return carry

    lax.fori_loop(0, qi, body, 0)

    kb0 = qi * FOX_TILES
    kk = lax.broadcasted_iota(jnp.int32, (FOX_BK, FOX_BK), 0)
    qq = lax.broadcasted_iota(jnp.int32, (FOX_BK, FOX_BK), 1)
    s = s_ref[...]
    for d in range(FOX_TILES):
        lo = d * FOX_BK
        if d + 1 < FOX_TILES:
            s_next = scores(kb0 + d + 1, q_aug[:, lo + FOX_BK:])
        edge = jnp.where(kk <= qq, s[:, :FOX_BK], -MASKED)
        s = edge if d + 1 == FOX_TILES else jnp.concatenate([edge, s[:, FOX_BK:]], axis=1)
        m, l, alpha, p = softmax_step(s, m_ref[:, lo:], l_ref[:, lo:])
        acc_ref[:, lo:] = alpha * acc_ref[:, lo:] + pv(kb0 + d, p)
        m_ref[:, lo:] = m
        l_ref[:, lo:] = l
        if d + 1 < FOX_TILES:
            s = s_next
    o = acc_ref[...] / l_ref[...]
    o_ref[0] = o.T.astype(bf16)


def fox_attention(qT, c_pieces, k, vT):
    nq = SEQ // FOX_BQ
    nk = SEQ // FOX_BK
    cspec = pl.BlockSpec((1, SEQ, HEAD_DIM), lambda b, h, i: (b, 0, 0))
    return pl.pallas_call(
        _fox_attn_kernel,
        out_shape=jax.ShapeDtypeStruct((BATCH, SEQ, D_MODEL), bf16),
        grid=(BATCH, FOX_HEADS, nq),
        in_specs=[pl.BlockSpec((1, 1, HEAD_DIM, FOX_BQ), lambda b, h, i: (b, h, 0, i)),
                  cspec, cspec, cspec,
                  pl.BlockSpec((1, 1, SEQ, HEAD_DIM), lambda b, h, i: (b, h, 0, 0)),
                  pl.BlockSpec((1, 1, nk, HEAD_DIM, FOX_BK), lambda b, h, i: (b, h, 0, 0, 0))],
        out_specs=pl.BlockSpec((1, FOX_BQ, HEAD_DIM), lambda b, h, i: (b, i, h)),
        scratch_shapes=[pltpu.VMEM((SEQ, AUG), bf16),
                        pltpu.VMEM((FOX_BK, FOX_BQ), f32),
                        pltpu.VMEM((1, FOX_BQ), f32),
                        pltpu.VMEM((1, FOX_BQ), f32),
                        pltpu.VMEM((HEAD_DIM, FOX_BQ), f32)],
        compiler_params=_params(("arbitrary",) * 3, 48),
        name="fox_attention",
    )(qT, *c_pieces, k, vT)


OUT_BM = 512
OUT_SLAB = 512
LANES = 128
DIL_QUARTER = OUT_BM // DIL_RES


def _rmsnorm_rows(y, g):
    ms = jnp.mean(y * y, axis=-1, keepdims=True)
    return y * lax.rsqrt(ms + EPS) * g


def _outproj_kernel(o_ref, w_ref, x_ref, g_ref, y_ref, h_ref):
    for s in range(D_MODEL // OUT_SLAB):
        cols = slice(s * OUT_SLAB, (s + 1) * OUT_SLAB)
        y_ref[:, cols] = x_ref[:, cols] + jnp.dot(o_ref[...], w_ref[:, cols], preferred_element_type=f32)
    h_ref[...] = _rmsnorm_rows(y_ref[...], g_ref[...]).astype(bf16)


def _dil_outproj_kernel(o_ref, w_ref, x_ref, g_ref, y_ref, h_ref, nat_ref):
    o = o_ref[0].reshape(OUT_BM, D_MODEL)
    per = MXU_N // LANES
    for s in range(D_MODEL // MXU_N):
        mix = jnp.dot(o, w_ref[:, s * MXU_N:(s + 1) * MXU_N], preferred_element_type=f32)
        for u in range(per):
            c = s * per + u
            for rho in range(DIL_RES):
                nat_ref[c, pl.ds(rho, DIL_QUARTER, stride=DIL_RES), :] = (
                    mix[rho * DIL_QUARTER:(rho + 1) * DIL_QUARTER, u * LANES:(u + 1) * LANES])
            cols = slice(c * LANES, (c + 1) * LANES)
            y_ref[:, cols] = x_ref[:, cols] + nat_ref[c]
    h_ref[...] = _rmsnorm_rows(y_ref[...], g_ref[...]).astype(bf16)


def outproj_residual(o, w, x2d, g, residue_major):
    row = pl.BlockSpec((OUT_BM, D_MODEL), lambda i: (i, 0))
    if residue_major:
        quarters = DIL_TILE // OUT_BM
        o = o.reshape(TOKENS // DIL_TILE, DIL_RES, DIL_ROWS, D_MODEL)
        ospec = pl.BlockSpec((1, DIL_RES, DIL_QUARTER, D_MODEL), lambda i: (i // quarters, 0, i % quarters, 0))
        body, scratch = _dil_outproj_kernel, [pltpu.VMEM((D_MODEL // LANES, OUT_BM, LANES), f32)]
    else:
        ospec, body, scratch = row, _outproj_kernel, []
    return pl.pallas_call(
        body,
        out_shape=(jax.ShapeDtypeStruct((TOKENS, D_MODEL), f32), jax.ShapeDtypeStruct((TOKENS, D_MODEL), bf16)),
        grid=(TOKENS // OUT_BM,),
        in_specs=[ospec,
                  pl.BlockSpec((D_MODEL, D_MODEL), lambda i: (0, 0)),
                  row,
                  pl.BlockSpec((1, D_MODEL), lambda i: (0, 0))],
        out_specs=(row, row),
        scratch_shapes=scratch,
        compiler_params=_params(("arbitrary",), 52),
        name="dil_outproj_residual" if residue_major else "outproj_residual",
    )(o, w, x2d, g.reshape(1, D_MODEL))


MLP_BM = OUT_BM
MLP_BF = 1024


def _mlp_kernel(h_ref, wu_ref, wd_ref, x_ref, *refs, next_h):
    if next_h is None:
        (y_ref,) = refs
    else:
        g_ref, y_ref, hn_ref, slab_ref = refs
    f = pl.program_id(1)
    a = jnp.maximum(jnp.dot(h_ref[...], wu_ref[...], preferred_element_type=f32), 0.0)
    upd = jnp.dot((a * a).astype(bf16), wd_ref[...], preferred_element_type=f32)

    @pl.when(f == 0)
    def _():
        y_ref[...] = x_ref[...] + upd

    @pl.when(f > 0)
    def _():
        y_ref[...] += upd

    if next_h is not None:
        @pl.when(f == pl.num_programs(1) - 1)
        def _():
            hn = _rmsnorm_rows(y_ref[...], g_ref[...])
            for c in range(D_MODEL // LANES):
                slab_ref[c] = hn[:, c * LANES:(c + 1) * LANES]
            for rho in range(DIL_RES):
                for c in range(D_MODEL // LANES):
                    hn_ref[0, rho, :, c * LANES:(c + 1) * LANES] = (
                        slab_ref[c, pl.ds(rho, DIL_QUARTER, stride=DIL_RES), :].astype(bf16))


def mlp_residual(h2d, w_up, w_down, x2d, g_next=None):
    row = pl.BlockSpec((MLP_BM, D_MODEL), lambda i, f: (i, 0))
    in_specs = [row,
                pl.BlockSpec((D_MODEL, MLP_BF), lambda i, f: (0, f)),
                pl.BlockSpec((MLP_BF, D_MODEL), lambda i, f: (f, 0)),
                row]
    y_shape = jax.ShapeDtypeStruct((TOKENS, D_MODEL), f32)
    if g_next is None:
        args, out_shape, out_specs, scratch = (h2d, w_up, w_down, x2d), y_shape, row, []
    else:
        quarters = DIL_TILE // MLP_BM
        in_specs.append(pl.BlockSpec((1, D_MODEL), lambda i, f: (0, 0)))
        args = (h2d, w_up, w_down, x2d, g_next.reshape(1, D_MODEL))
        out_shape = (y_shape, jax.ShapeDtypeStruct((TOKENS // DIL_TILE, DIL_RES, DIL_ROWS, D_MODEL), bf16))
        out_specs = (row, pl.BlockSpec((1, DIL_RES, DIL_QUARTER, D_MODEL),
                                       lambda i, f: (i // quarters, 0, i % quarters, 0)))
        scratch = [pltpu.VMEM((D_MODEL // LANES, MLP_BM, LANES), f32)]
    return pl.pallas_call(
        functools.partial(_mlp_kernel, next_h=None if g_next is None else "residue_major"),
        out_shape=out_shape,
        grid=(TOKENS // MLP_BM, D_FF // MLP_BF),
        in_specs=in_specs,
        out_specs=out_specs,
        scratch_shapes=scratch,
        compiler_params=_params(("arbitrary", "arbitrary"), 56),
        name="mlp_residual",
    )(*args)


DIL_COLS = 2 * N_GROUPS * DIL_HEADS * HEAD_DIM + DIL_HEADS * DIL_V_DIM
DIL_QK_BLOCKS = 2 * N_GROUPS


def _dil_proj_kernel(h_ref, w_ref, g_ref, o_ref):
    j = pl.program_id(1)
    h = h_ref[...]

    def slabs(normalise):
        for s in range(PROJ_BN // MXU_N):
            y2 = jnp.dot(h, w_ref[:, s * MXU_N:(s + 1) * MXU_N], preferred_element_type=f32)
            if normalise:
                for u in range(MXU_N // HEAD_DIM):
                    cols = slice(s * MXU_N + u * HEAD_DIM, s * MXU_N + (u + 1) * HEAD_DIM)
                    o_ref[:, cols] = _head_rmsnorm(y2[:, u * HEAD_DIM:(u + 1) * HEAD_DIM], g_ref[0]).astype(bf16)
            else:
                o_ref[:, s * MXU_N:(s + 1) * MXU_N] = y2.astype(bf16)

    @pl.when(j < DIL_QK_BLOCKS)
    def _():
        slabs(True)

    @pl.when(j >= DIL_QK_BLOCKS)
    def _():
        slabs(False)


def dil_proj(h2d, w, gains):
    out = pl.pallas_call(
        _dil_proj_kernel,
        out_shape=jax.ShapeDtypeStruct((TOKENS, DIL_COLS), bf16),
        grid=(TOKENS // PROJ_BM, DIL_COLS // PROJ_BN),
        in_specs=[pl.BlockSpec((PROJ_BM, D_MODEL), lambda i, j: (i, 0)),
                  pl.BlockSpec((D_MODEL, PROJ_BN), lambda i, j: (0, j)),
                  pl.BlockSpec((1, 1, HEAD_DIM), lambda i, j: (j, 0, 0))],
        out_specs=pl.BlockSpec((PROJ_BM, PROJ_BN), lambda i, j: (i, j)),
        compiler_params=_params(("arbitrary", "arbitrary"), 48),
        name="dil_proj",
    )(h2d, w, gains)
    return out.reshape(BATCH, SEQ, DIL_COLS)


def _dil_distance_tables():
    def table(delta, prev_half, r):
        valid = (delta >= 0) & (delta <= DIL_SPAN)
        d = np.where(valid, delta * (r * LOG2E), MASKED).astype(np.float32)
        first = np.where(prev_half, np.float32(MASKED), d).astype(np.float32)
        return d, first

    i = np.arange(128)[:, None]
    j = np.arange(256)[None, :]
    t16 = table(128 + i - j, j < 128, 16)
    c, u = i // 32, i % 32
    cp, half, up = j // 64, (j % 64) // 32, j % 32
    t4 = table(128 + 4 * u + c - 128 * half - 4 * up - cp, half == 0, 4)
    i = np.arange(256)[:, None]
    j = np.arange(512)[None, :]
    rho, u = i // 16, i % 16
    rp, half, up = j // 32, (j % 32) // 16, j % 16
    t1 = table(256 + 16 * u + rho - 256 * half - 16 * up - rp, half == 0, 1)
    return t1, t4, t16


def _dil_attn_kernel(slopes_ref, q0_ref, q1_ref, q2_ref, k0_ref, k1_ref, k2_ref, v_ref,
                     d1_ref, d1f_ref, d4_ref, d4f_ref, d16_ref, d16f_ref,
                     o_ref, m_ref, l_ref, acc_ref):
    h = pl.program_id(1)
    t = pl.program_id(2)
    base = pl.multiple_of(t * DIL_TILE, DIL_TILE)
    pbase = pl.multiple_of(jnp.maximum(t - 1, 0) * DIL_TILE, DIL_TILE)
    first = t == 0

    def attend(q, kw, vw, bias, rows, size, init):
        s = lax.dot_general(q, kw, (((1,), (1,)), ((), ())), preferred_element_type=f32) + bias
        m_blk = jnp.max(s, axis=1, keepdims=True)
        if init:
            m_new = m_blk
        else:
            m_old = jnp.concatenate([m_ref[r:r + size, :] for r in rows], axis=0)
            l_old = jnp.concatenate([l_ref[r:r + size, :] for r in rows], axis=0)
            o_old = jnp.concatenate([acc_ref[r:r + size, :] for r in rows], axis=0)
            m_new = jnp.maximum(m_old, m_blk)
        p = jnp.exp2(s - m_new)
        l_new = jnp.sum(p, axis=1, keepdims=True)
        o_new = jnp.dot(p.astype(bf16), vw, preferred_element_type=f32)
        if not init:
            alpha = jnp.exp2(m_old - m_new)
            l_new = alpha * l_old + l_new
            o_new = alpha * o_old + o_new
        for c, r in enumerate(rows):
            m_ref[r:r + size, :] = m_new[c * size:(c + 1) * size]
            l_ref[r:r + size, :] = l_new[c * size:(c + 1) * size]
            acc_ref[r:r + size, :] = o_new[c * size:(c + 1) * size]

    def window(ref, rows, size, n):
        if n > 0:
            return jnp.concatenate([ref[0, pl.ds(base + (r - size), 2 * size), :] for r in rows], axis=0)
        parts = []
        for r in rows:
            parts.append(ref[0, pl.ds(pbase + (r + DIL_ROWS - size), size), :])
            parts.append(ref[0, pl.ds(base + r, size), :])
        return jnp.concatenate(parts, axis=0)

    slope = slopes_ref[2 * DIL_HEADS + h]
    bias = -slope * jnp.where(first, d16f_ref[...], d16_ref[...])
    for rho in range(DIL_RES):
        rows = [rho * DIL_ROWS]
        q = q2_ref[0, rows[0]:rows[0] + DIL_ROWS, :]
        attend(q, window(k2_ref, rows, DIL_ROWS, 0), window(v_ref, rows, DIL_ROWS, 0),
               bias, rows, DIL_ROWS, True)

    slope = slopes_ref[DIL_HEADS + h]
    bias_in = -slope * d4_ref[...]
    bias_edge = -slope * jnp.where(first, d4f_ref[...], d4_ref[...])
    for rho4 in range(4):
        for n in range(4):
            rows = [(rho4 + 4 * c) * DIL_ROWS + 32 * n for c in range(4)]
            q = jnp.concatenate([q1_ref[0, r:r + 32, :] for r in rows], axis=0)
            attend(q, window(k1_ref, rows, 32, n), window(v_ref, rows, 32, n),
                   bias_edge if n == 0 else bias_in, rows, 32, False)

    slope = slopes_ref[h]
    bias_in = -slope * d1_ref[...]
    bias_edge = -slope * jnp.where(first, d1f_ref[...], d1_ref[...])
    for n in range(DIL_ROWS // 16):
        rows = [rho * DIL_ROWS + 16 * n for rho in range(DIL_RES)]
        q = jnp.concatenate([q0_ref[0, r:r + 16, :] for r in rows], axis=0)
        attend(q, window(k0_ref, rows, 16, n), window(v_ref, rows, 16, n),
               bias_edge if n == 0 else bias_in, rows, 16, False)

    o_ref[0] = (acc_ref[...] / l_ref[...]).astype(bf16)


def dil_attention(p3, slopes):
    tables = [jnp.asarray(a) for pair in _dil_distance_tables() for a in pair]
    gh = DIL_HEADS
    kcol0 = N_GROUPS * gh
    vcol0 = 2 * N_GROUPS * gh * HEAD_DIM // DIL_V_DIM
    qspecs = [pl.BlockSpec((1, DIL_TILE, HEAD_DIM), lambda b, h, t, g=g: (b, t, g * gh + h))
              for g in range(N_GROUPS)]
    kspecs = [pl.BlockSpec((1, SEQ, HEAD_DIM), lambda b, h, t, g=g: (b, 0, kcol0 + g * gh + h))
              for g in range(N_GROUPS)]
    vspec = pl.BlockSpec((1, SEQ, DIL_V_DIM), lambda b, h, t: (b, 0, vcol0 + h))
    tspecs = [pl.BlockSpec(a.shape, lambda b, h, t: (0, 0)) for a in tables]
    return pl.pallas_call(
        _dil_attn_kernel,
        out_shape=jax.ShapeDtypeStruct((BATCH, SEQ, D_MODEL), bf16),
        grid=(BATCH, DIL_HEADS, SEQ // DIL_TILE),
        in_specs=[pl.BlockSpec(memory_space=pltpu.SMEM)] + qspecs + kspecs + [vspec] + tspecs,
        out_specs=pl.BlockSpec((1, DIL_TILE, DIL_V_DIM), lambda b, h, t: (b, t, h)),
        scratch_shapes=[pltpu.VMEM((DIL_TILE, 1), f32),
                        pltpu.VMEM((DIL_TILE, 1), f32),
                        pltpu.VMEM((DIL_TILE, DIL_V_DIM), f32)],
        compiler_params=_params(("arbitrary",) * 3, 48),
        name="dil_attention",
    )(slopes, p3, p3, p3, p3, p3, p3, p3, *tables)


def _fox_mixer(h2d, w_in, b_f, q_gain, k_gain):
    h3 = h2d.reshape(BATCH, SEQ, D_MODEL)
    wb = w_in.astype(bf16)
    qkv_cols = 3 * FOX_HEADS * HEAD_DIM
    wf = jnp.pad(wb[:, qkv_cols:], ((0, 0), (0, HEAD_DIM - FOX_HEADS)))
    bfp = jnp.pad(b_f.astype(f32), (0, HEAD_DIM - FOX_HEADS)).reshape(1, HEAD_DIM)
    c_pieces = fox_gate(h3, wf, bfp)

    nb = D_MODEL // PROJ_BN
    qg = (q_gain * (QK_SCALE * LOG2E)).astype(f32).reshape(1, HEAD_DIM)
    kg = k_gain.astype(f32).reshape(1, HEAD_DIM)
    qT = fox_proj(h3, wb, qg, "q", 0)
    k = fox_proj(h3, wb, kg, "k", nb)
    vT = fox_proj(h3, wb, kg, "v", 2 * nb)
    return fox_attention(qT, c_pieces, k, vT).reshape(TOKENS, D_MODEL)


def _dil_mixer(h_rm, w_in, q_gain, k_gain):
    ones = jnp.ones((DIL_COLS // PROJ_BN - DIL_QK_BLOCKS, HEAD_DIM), f32)
    gains = jnp.concatenate([q_gain.astype(f32) * (QK_SCALE * LOG2E), k_gain.astype(f32), ones], axis=0)
    p3 = dil_proj(h_rm, w_in.astype(bf16), gains.reshape(-1, 1, HEAD_DIM))
    n_heads = N_GROUPS * DIL_HEADS
    slopes = jnp.exp2(-ALIBI_MAX_EXP * jnp.arange(1, n_heads + 1, dtype=f32) / n_heads)
    return dil_attention(p3, slopes)


def kernel(x, fox_w_in, fox_b_f, fox_q_gain, fox_k_gain, fox_w_out, dil_w_in, dil_q_gain, dil_k_gain,
           dil_w_out, mix_norm_g, mlp_norm_g, mlp_w_up, mlp_w_down):
    x0 = x.reshape(TOKENS, D_MODEL)
    o = _fox_mixer(rmsnorm(x0, mix_norm_g[0]), fox_w_in[0], fox_b_f[0], fox_q_gain[0], fox_k_gain[0])
    x1, h = outproj_residual(o, fox_w_out[0].astype(bf16), x0, mlp_norm_g[0], residue_major=False)
    x2, h_rm = mlp_residual(h, mlp_w_up[0].astype(bf16), mlp_w_down[0].astype(bf16), x1, g_next=mix_norm_g[1])
    o = _dil_mixer(h_rm.reshape(TOKENS, D_MODEL), dil_w_in[0], dil_q_gain[0], dil_k_gain[0])
    x3, h = outproj_residual(o, dil_w_out[0].astype(bf16), x2, mlp_norm_g[1], residue_major=True)
    x4 = mlp_residual(h, mlp_w_up[1].astype(bf16), mlp_w_down[1].astype(bf16), x3)
    return x4.reshape(BATCH, SEQ, D_MODEL)
```

```python
import functools

import numpy as np
import jax
import jax.numpy as jnp
from jax import lax
from jax.experimental import pallas as pl
from jax.experimental.pallas import tpu as pltpu

f32 = jnp.float32
bf16 = jnp.bfloat16

BATCH = 2
SEQ = 8192
D_MODEL = 2048
TOKENS = BATCH * SEQ
HEAD_DIM = 128
EPS = 1e-6
D_FF = 4 * D_MODEL
FOX_HEADS = D_MODEL // HEAD_DIM
DIL_PATTERNS = ((128, 1), (512, 4), (2048, 16))
N_GROUPS = len(DIL_PATTERNS)
DIL_SPAN = 128
DIL_HEADS = D_MODEL // (2 * HEAD_DIM)
DIL_V_DIM = D_MODEL // DIL_HEADS
ALIBI_MAX_EXP = 8.0

LOG2E = 1.4426950408889634
QK_SCALE = HEAD_DIM ** -0.5
MASKED = 1e30
MIB = 1024 * 1024

DIL_TILE = 2048
DIL_RES = 16
DIL_ROWS = DIL_TILE // DIL_RES


def _params(semantics, vmem_mib):
    return pltpu.CompilerParams(dimension_semantics=semantics, vmem_limit_bytes=vmem_mib * MIB)


def _rmsnorm_kernel(x_ref, g_ref, o_ref):
    x = x_ref[...]
    ms = jnp.mean(x * x, axis=-1, keepdims=True)
    o_ref[...] = (x * lax.rsqrt(ms + EPS) * g_ref[...]).astype(o_ref.dtype)


def rmsnorm(x2d, g):
    bm = 512
    return pl.pallas_call(
        _rmsnorm_kernel,
        out_shape=jax.ShapeDtypeStruct((TOKENS, D_MODEL), bf16),
        grid=(TOKENS // bm,),
        in_specs=[pl.BlockSpec((bm, D_MODEL), lambda i: (i, 0)),
                  pl.BlockSpec((1, D_MODEL), lambda i: (0, 0))],
        out_specs=pl.BlockSpec((bm, D_MODEL), lambda i: (i, 0)),
        compiler_params=_params(("arbitrary",), 32),
        name="rmsnorm",
    )(x2d, g.reshape(1, D_MODEL))


def _cast_kernel(w_ref, o_ref):
    o_ref[...] = w_ref[...].astype(o_ref.dtype)


def to_bf16(w3, layer):
    _, rows, cols = w3.shape
    br = 256 if cols > 4096 else 1024
    return pl.pallas_call(
        _cast_kernel,
        out_shape=jax.ShapeDtypeStruct((rows, cols), bf16),
        grid=(rows // br,),
        in_specs=[pl.BlockSpec((None, br, cols), lambda i: (layer, i, 0))],
        out_specs=pl.BlockSpec((br, cols), lambda i: (i, 0)),
        compiler_params=_params(("arbitrary",), 40),
        name="to_bf16",
    )(w3)


def _head_rmsnorm(y, gain):
    ms = jnp.mean(y * y, axis=-1, keepdims=True)
    return y * lax.rsqrt(ms + EPS) * gain


GATE_BM = 512


def _gate_kernel(h_ref, wf_ref, bf_ref, hi_ref, mid_ref, lo_ref, carry_ref):
    @pl.when(pl.program_id(1) == 0)
    def _():
        carry_ref[...] = jnp.zeros_like(carry_ref)

    f = jnp.dot(h_ref[0], wf_ref[...], preferred_element_type=f32) + bf_ref[...]
    lf = jnp.minimum(f, 0.0) - jnp.log1p(jnp.exp(-jnp.abs(f)))
    row = lax.broadcasted_iota(jnp.int32, (GATE_BM, GATE_BM), 0)
    col = lax.broadcasted_iota(jnp.int32, (GATE_BM, GATE_BM), 1)
    tri = (col <= row).astype(bf16)
    p0 = lf.astype(bf16)
    r0 = lf - p0.astype(f32)
    p1 = r0.astype(bf16)
    p2 = (r0 - p1.astype(f32)).astype(bf16)
    cs = (jnp.dot(tri, p0, preferred_element_type=f32)
          + jnp.dot(tri, p1, preferred_element_type=f32)
          + jnp.dot(tri, p2, preferred_element_type=f32))
    c = cs + carry_ref[...]
    carry_ref[...] = c[GATE_BM - 1:GATE_BM, :]
    c2 = c * LOG2E
    hi = c2.astype(bf16)
    r1 = c2 - hi.astype(f32)
    mid = r1.astype(bf16)
    lo = (r1 - mid.astype(f32)).astype(bf16)
    hi_ref[0] = hi
    mid_ref[0] = mid
    lo_ref[0] = lo


def fox_gate(h3, wf, bfp):
    piece = jax.ShapeDtypeStruct((BATCH, SEQ, HEAD_DIM), bf16)
    spec = pl.BlockSpec((1, GATE_BM, HEAD_DIM), lambda b, s: (b, s, 0))
    return pl.pallas_call(
        _gate_kernel,
        out_shape=(piece, piece, piece),
        grid=(BATCH, SEQ // GATE_BM),
        in_specs=[pl.BlockSpec((1, GATE_BM, D_MODEL), lambda b, s: (b, s, 0)),
                  pl.BlockSpec((D_MODEL, HEAD_DIM), lambda b, s: (0, 0)),
                  pl.BlockSpec((1, HEAD_DIM), lambda b, s: (0, 0))],
        out_specs=(spec, spec, spec),
        scratch_shapes=[pltpu.VMEM((1, HEAD_DIM), f32)],
        compiler_params=_params(("arbitrary", "arbitrary"), 32),
        name="fox_gate",
    )(h3, wf, bfp)


PROJ_BM = 1024
PROJ_BN = 1024
PROJ_HEADS = PROJ_BN // HEAD_DIM
MXU_N = 256
FOX_BK = 256
FOX_V_ROWS = HEAD_DIM + 16


def _fox_proj_kernel(h_ref, w_ref, g_ref, o_ref, *, mode):
    h = h_ref[0]
    if mode == "v":
        extra = lax.broadcasted_iota(jnp.int32, (FOX_V_ROWS - HEAD_DIM, FOX_BK), 0)
        ones_row = jnp.where(extra == 0, 1.0, 0.0).astype(bf16)
    for s in range(PROJ_BN // MXU_N):
        y2 = jnp.dot(h, w_ref[:, s * MXU_N:(s + 1) * MXU_N], preferred_element_type=f32)
        for u in range(MXU_N // HEAD_DIM):
            hh = s * (MXU_N // HEAD_DIM) + u
            y = y2[:, u * HEAD_DIM:(u + 1) * HEAD_DIM]
            if mode == "q":
                o_ref[0, hh] = _head_rmsnorm(y, g_ref[...]).T.astype(bf16)
            elif mode == "k":
                o_ref[0, hh] = _head_rmsnorm(y, g_ref[...]).astype(bf16)
            else:
                for c in range(PROJ_BM // FOX_BK):
                    o_ref[0, hh, c, :HEAD_DIM, :] = y[c * FOX_BK:(c + 1) * FOX_BK, :].T.astype(bf16)
                    o_ref[0, hh, c, HEAD_DIM:, :] = ones_row


def fox_proj(h3, w, gain, mode, col_block0):
    nsb = SEQ // PROJ_BM
    if mode == "q":
        shape = (BATCH, FOX_HEADS, HEAD_DIM, SEQ)
        ospec = pl.BlockSpec((1, PROJ_HEADS, HEAD_DIM, PROJ_BM), lambda b, s, j: (b, j, 0, s))
    elif mode == "k":
        shape = (BATCH, FOX_HEADS, SEQ, HEAD_DIM)
        ospec = pl.BlockSpec((1, PROJ_HEADS, PROJ_BM, HEAD_DIM), lambda b, s, j: (b, j, s, 0))
    else:
        shape = (BATCH, FOX_HEADS, SEQ // FOX_BK, FOX_V_ROWS, FOX_BK)
        ospec = pl.BlockSpec((1, PROJ_HEADS, PROJ_BM // FOX_BK, FOX_V_ROWS, FOX_BK),
                             lambda b, s, j: (b, j, s, 0, 0))
    return pl.pallas_call(
        functools.partial(_fox_proj_kernel, mode=mode),
        out_shape=jax.ShapeDtypeStruct(shape, bf16),
        grid=(BATCH, nsb, D_MODEL // PROJ_BN),
        in_specs=[pl.BlockSpec((1, PROJ_BM, D_MODEL), lambda b, s, j: (b, s, 0)),
                  pl.BlockSpec((D_MODEL, PROJ_BN), lambda b, s, j: (0, col_block0 + j)),
                  pl.BlockSpec((1, HEAD_DIM), lambda b, s, j: (0, 0))],
        out_specs=ospec,
        compiler_params=_params(("arbitrary",) * 3, 48),
        name="fox_proj_" + mode,
    )(h3, w, gain)


FOX_BQ = 1024
FOX_TILES = FOX_BQ // FOX_BK
AUG = 2 * HEAD_DIM
AUG_ROWS = 16
AUG_CHUNK = 1024


def _fox_attn_kernel(qT_ref, chi_ref, cmid_ref, clo_ref, k_ref, vT_ref, o_ref,
                     kaug_ref, s_ref, m_ref, acc_ref):
    h = pl.program_id(1)
    qi = pl.program_id(2)
    pieces = (chi_ref, cmid_ref, clo_ref)
    n_p = len(pieces)

    @pl.when(qi == 0)
    def _():
        kaug_ref[:, :HEAD_DIM] = k_ref[0, 0]
        row = lax.broadcasted_iota(jnp.int32, (HEAD_DIM, HEAD_DIM), 0)
        col = lax.broadcasted_iota(jnp.int32, (HEAD_DIM, HEAD_DIM), 1)
        lane = lax.broadcasted_iota(jnp.int32, (AUG_CHUNK, HEAD_DIM), 1)
        for c in range(SEQ // AUG_CHUNK):
            rows = slice(c * AUG_CHUNK, (c + 1) * AUG_CHUNK)
            aug = jnp.where(lane < n_p, 1.0, 0.0)
            for p, ref in enumerate(pieces):
                sel = jnp.where((row == h) & (col == n_p + p), -1.0, 0.0).astype(bf16)
                aug = aug + jnp.dot(ref[0, rows, :], sel, preferred_element_type=f32)
            kaug_ref[rows, HEAD_DIM:] = aug.astype(bf16)

    q0 = pl.multiple_of(qi * FOX_BQ, FOX_BQ)
    row = lax.broadcasted_iota(jnp.int32, (AUG_ROWS, HEAD_DIM), 0)
    col = lax.broadcasted_iota(jnp.int32, (AUG_ROWS, HEAD_DIM), 1)
    slot = lax.broadcasted_iota(jnp.int32, (AUG_ROWS, FOX_BQ), 0)
    qa = jnp.where((slot >= n_p) & (slot < 2 * n_p), 1.0, 0.0)
    for p, ref in enumerate(pieces):
        sel = jnp.where((row == p) & (col == h), 1.0, 0.0).astype(bf16)
        qa = qa + lax.dot_general(sel, ref[0, pl.ds(q0, FOX_BQ), :], (((1,), (1,)), ((), ())),
                                  preferred_element_type=f32)
    q_aug = jnp.concatenate([qT_ref[0, 0], qa.astype(bf16),
                             jnp.zeros((HEAD_DIM - AUG_ROWS, FOX_BQ), bf16)], axis=0)

    def scores(kb, q):
        start = pl.multiple_of(kb * FOX_BK, FOX_BK)
        return jnp.dot(kaug_ref[pl.ds(start, FOX_BK), :], q, preferred_element_type=f32)

    def softmax_step(s, m):
        m_new = jnp.maximum(m, jnp.max(s, axis=0, keepdims=True))
        return m_new, jnp.exp2(m - m_new), jnp.exp2(s - m_new).astype(bf16)

    def pv(kb, p):
        return jnp.dot(vT_ref[0, 0, kb], p, preferred_element_type=f32)

    s_ref[...] = scores(0, q_aug)
    m_ref[...] = jnp.full_like(m_ref, -MASKED)
    acc_ref[...] = jnp.zeros_like(acc_ref)

    def body(j, carry):
        kb0 = j * FOX_TILES
        s, m, acc = s_ref[...], m_ref[...], acc_ref[...]
        for u in range(FOX_TILES):
            s_next = scores(kb0 + u + 1, q_aug)
            m, alpha, p = softmax_step(s, m)
            acc = alpha * acc + pv(kb0 + u, p)
            s = s_next
        s_ref[...], m_ref[...], acc_ref[...] = s, m, acc
        return carry

    lax.fori_loop(0, qi, body, 0)

    kb0 = qi * FOX_TILES
    kk = lax.broadcasted_iota(jnp.int32, (FOX_BK, FOX_BK), 0)
    qq = lax.broadcasted_iota(jnp.int32, (FOX_BK, FOX_BK), 1)
    s = s_ref[...]
    for d in range(FOX_TILES):
        lo = d * FOX_BK
        if d + 1 < FOX_TILES:
            s_next = scores(kb0 + d + 1, q_aug[:, lo + FOX_BK:])
        edge = jnp.where(kk <= qq, s[:, :FOX_BK], -MASKED)
        s = edge if d + 1 == FOX_TILES else jnp.concatenate([edge, s[:, FOX_BK:]], axis=1)
        m, alpha, p = softmax_step(s, m_ref[:, lo:])
        acc_ref[:, lo:] = alpha * acc_ref[:, lo:] + pv(kb0 + d, p)
        m_ref[:, lo:] = m
        if d + 1 < FOX_TILES:
            s = s_next
    o = acc_ref[:HEAD_DIM, :] / acc_ref[HEAD_DIM:HEAD_DIM + 1, :]
    o_ref[0] = o.T.astype(bf16)


def fox_attention(qT, c_pieces, k, vT):
    nq = SEQ // FOX_BQ
    nk = SEQ // FOX_BK
    cspec = pl.BlockSpec((1, SEQ, HEAD_DIM), lambda b, h, i: (b, 0, 0))
    return pl.pallas_call(
        _fox_attn_kernel,
        out_shape=jax.ShapeDtypeStruct((BATCH, SEQ, D_MODEL), bf16),
        grid=(BATCH, FOX_HEADS, nq),
        in_specs=[pl.BlockSpec((1, 1, HEAD_DIM, FOX_BQ), lambda b, h, i: (b, h, 0, i)),
                  cspec, cspec, cspec,
                  pl.BlockSpec((1, 1, SEQ, HEAD_DIM), lambda b, h, i: (b, h, 0, 0)),
                  pl.BlockSpec((1, 1, nk, FOX_V_ROWS, FOX_BK), lambda b, h, i: (b, h, 0, 0, 0))],
        out_specs=pl.BlockSpec((1, FOX_BQ, HEAD_DIM), lambda b, h, i: (b, i, h)),
        scratch_shapes=[pltpu.VMEM((SEQ, AUG), bf16),
                        pltpu.VMEM((FOX_BK, FOX_BQ), f32),
                        pltpu.VMEM((1, FOX_BQ), f32),
                        pltpu.VMEM((FOX_V_ROWS, FOX_BQ), f32)],
        compiler_params=_params(("arbitrary",) * 3, 48),
        name="fox_attention",
    )(qT, *c_pieces, k, vT)


OUT_BM = 512
OUT_SLAB = 512
LANES = 128
DIL_QUARTER = OUT_BM // DIL_RES


def _rmsnorm_rows(y, g):
    ms = jnp.mean(y * y, axis=-1, keepdims=True)
    return y * lax.rsqrt(ms + EPS) * g


def _outproj_kernel(o_ref, w_ref, x_ref, g_ref, y_ref, h_ref):
    for s in range(D_MODEL // OUT_SLAB):
        cols = slice(s * OUT_SLAB, (s + 1) * OUT_SLAB)
        y_ref[:, cols] = x_ref[:, cols] + jnp.dot(o_ref[...], w_ref[:, cols], preferred_element_type=f32)
    h_ref[...] = _rmsnorm_rows(y_ref[...], g_ref[...]).astype(bf16)


def _dil_outproj_kernel(o_ref, w_ref, x_ref, g_ref, y_ref, h_ref, nat_ref):
    o = o_ref[0].reshape(OUT_BM, D_MODEL)
    per = MXU_N // LANES
    for s in range(D_MODEL // MXU_N):
        mix = jnp.dot(o, w_ref[:, s * MXU_N:(s + 1) * MXU_N], preferred_element_type=f32)
        for u in range(per):
            c = s * per + u
            for rho in range(DIL_RES):
                nat_ref[c, pl.ds(rho, DIL_QUARTER, stride=DIL_RES), :] = (
                    mix[rho * DIL_QUARTER:(rho + 1) * DIL_QUARTER, u * LANES:(u + 1) * LANES])
            cols = slice(c * LANES, (c + 1) * LANES)
            y_ref[:, cols] = x_ref[:, cols] + nat_ref[c]
    h_ref[...] = _rmsnorm_rows(y_ref[...], g_ref[...]).astype(bf16)


def outproj_residual(o, w, x2d, g, residue_major):
    row = pl.BlockSpec((OUT_BM, D_MODEL), lambda i: (i, 0))
    if residue_major:
        quarters = DIL_TILE // OUT_BM
        o = o.reshape(TOKENS // DIL_TILE, DIL_RES, DIL_ROWS, D_MODEL)
        ospec = pl.BlockSpec((1, DIL_RES, DIL_QUARTER, D_MODEL), lambda i: (i // quarters, 0, i % quarters, 0))
        body, scratch = _dil_outproj_kernel, [pltpu.VMEM((D_MODEL // LANES, OUT_BM, LANES), f32)]
    else:
        ospec, body, scratch = row, _outproj_kernel, []
    return pl.pallas_call(
        body,
        out_shape=(jax.ShapeDtypeStruct((TOKENS, D_MODEL), f32), jax.ShapeDtypeStruct((TOKENS, D_MODEL), bf16)),
        grid=(TOKENS // OUT_BM,),
        in_specs=[ospec,
                  pl.BlockSpec((D_MODEL, D_MODEL), lambda i: (0, 0)),
                  row,
                  pl.BlockSpec((1, D_MODEL), lambda i: (0, 0))],
        out_specs=(row, row),
        scratch_shapes=scratch,
        compiler_params=_params(("arbitrary",), 52),
        name="dil_outproj_residual" if residue_major else "outproj_residual",
    )(o, w, x2d, g.reshape(1, D_MODEL))


MLP_BM = OUT_BM
MLP_BF = 1024


def _mlp_kernel(h_ref, wu_ref, wd_ref, x_ref, *refs, next_h):
    if next_h is None:
        (y_ref,) = refs
    else:
        g_ref, y_ref, hn_ref, slab_ref = refs
    f = pl.program_id(1)

    @pl.when(f == 0)
    def _():
        y_ref[...] = x_ref[...]

    a = jnp.maximum(jnp.dot(h_ref[...], wu_ref[...], preferred_element_type=f32), 0.0)
    y_ref[...] += jnp.dot((a * a).astype(bf16), wd_ref[...], preferred_element_type=f32)

    if next_h is not None:
        @pl.when(f == pl.num_programs(1) - 1)
        def _():
            hn = _rmsnorm_rows(y_ref[...], g_ref[...])
            for c in range(D_MODEL // LANES):
                slab_ref[c] = hn[:, c * LANES:(c + 1) * LANES]
            for rho in range(DIL_RES):
                for c in range(D_MODEL // LANES):
                    hn_ref[0, rho, :, c * LANES:(c + 1) * LANES] = (
                        slab_ref[c, pl.ds(rho, DIL_QUARTER, stride=DIL_RES), :].astype(bf16))


def mlp_residual(h2d, w_up, w_down, x2d, g_next=None):
    row = pl.BlockSpec((MLP_BM, D_MODEL), lambda i, f: (i, 0))
    in_specs = [row,
                pl.BlockSpec((D_MODEL, MLP_BF), lambda i, f: (0, f)),
                pl.BlockSpec((MLP_BF, D_MODEL), lambda i, f: (f, 0)),
                row]
    y_shape = jax.ShapeDtypeStruct((TOKENS, D_MODEL), f32)
    if g_next is None:
        args, out_shape, out_specs, scratch = (h2d, w_up, w_down, x2d), y_shape, row, []
    else:
        quarters = DIL_TILE // MLP_BM
        in_specs.append(pl.BlockSpec((1, D_MODEL), lambda i, f: (0, 0)))
        args = (h2d, w_up, w_down, x2d, g_next.reshape(1, D_MODEL))
        out_shape = (y_shape, jax.ShapeDtypeStruct((TOKENS // DIL_TILE, DIL_RES, DIL_ROWS, D_MODEL), bf16))
        out_specs = (row, pl.BlockSpec((1, DIL_RES, DIL_QUARTER, D_MODEL),
                                       lambda i, f: (i // quarters, 0, i % quarters, 0)))
        scratch = [pltpu.VMEM((D_MODEL // LANES, MLP_BM, LANES), f32)]
    return pl.pallas_call(
        functools.partial(_mlp_kernel, next_h=None if g_next is None else "residue_major"),
        out_shape=out_shape,
        grid=(TOKENS // MLP_BM, D_FF // MLP_BF),
        in_specs=in_specs,
        out_specs=out_specs,
        scratch_shapes=scratch,
        compiler_params=_params(("arbitrary", "arbitrary"), 56),
        name="mlp_residual",
    )(*args)


DIL_COLS = 2 * N_GROUPS * DIL_HEADS * HEAD_DIM + DIL_HEADS * DIL_V_DIM
DIL_QK_BLOCKS = 2 * N_GROUPS


def _dil_proj_kernel(h_ref, w_ref, g_ref, o_ref):
    j = pl.program_id(1)
    h = h_ref[...]

    def slabs(normalise):
        for s in range(PROJ_BN // MXU_N):
            y2 = jnp.dot(h, w_ref[:, s * MXU_N:(s + 1) * MXU_N], preferred_element_type=f32)
            if normalise:
                for u in range(MXU_N // HEAD_DIM):
                    cols = slice(s * MXU_N + u * HEAD_DIM, s * MXU_N + (u + 1) * HEAD_DIM)
                    o_ref[:, cols] = _head_rmsnorm(y2[:, u * HEAD_DIM:(u + 1) * HEAD_DIM], g_ref[0]).astype(bf16)
            else:
                o_ref[:, s * MXU_N:(s + 1) * MXU_N] = y2.astype(bf16)

    @pl.when(j < DIL_QK_BLOCKS)
    def _():
        slabs(True)

    @pl.when(j >= DIL_QK_BLOCKS)
    def _():
        slabs(False)


def dil_proj(h2d, w, gains):
    out = pl.pallas_call(
        _dil_proj_kernel,
        out_shape=jax.ShapeDtypeStruct((TOKENS, DIL_COLS), bf16),
        grid=(TOKENS // PROJ_BM, DIL_COLS // PROJ_BN),
        in_specs=[pl.BlockSpec((PROJ_BM, D_MODEL), lambda i, j: (i, 0)),
                  pl.BlockSpec((D_MODEL, PROJ_BN), lambda i, j: (0, j)),
                  pl.BlockSpec((1, 1, HEAD_DIM), lambda i, j: (j, 0, 0))],
        out_specs=pl.BlockSpec((PROJ_BM, PROJ_BN), lambda i, j: (i, j)),
        compiler_params=_params(("arbitrary", "arbitrary"), 48),
        name="dil_proj",
    )(h2d, w, gains)
    return out.reshape(BATCH, SEQ, DIL_COLS)


def _dil_distance_tables():
    def table(delta, prev_half, r):
        valid = (delta >= 0) & (delta <= DIL_SPAN)
        d = np.where(valid, delta * (r * LOG2E), MASKED).astype(np.float32)
        first = np.where(prev_half, np.float32(MASKED), d).astype(np.float32)
        return d, first

    i = np.arange(128)[:, None]
    j = np.arange(256)[None, :]
    t16 = table(128 + i - j, j < 128, 16)
    c, u = i // 32, i % 32
    cp, half, up = j // 64, (j % 64) // 32, j % 32
    t4 = table(128 + 4 * u + c - 128 * half - 4 * up - cp, half == 0, 4)
    i = np.arange(256)[:, None]
    j = np.arange(512)[None, :]
    rho, u = i // 16, i % 16
    rp, half, up = j // 32, (j % 32) // 16, j % 16
    t1 = table(256 + 16 * u + rho - 256 * half - 16 * up - rp, half == 0, 1)
    return t1, t4, t16


def _dil_attn_kernel(slopes_ref, q0_ref, q1_ref, q2_ref, k0_ref, k1_ref, k2_ref, v_ref,
                     d1_ref, d1f_ref, d4_ref, d4f_ref, d16_ref, d16f_ref,
                     o_ref, m_ref, l_ref, acc_ref):
    h = pl.program_id(1)
    t = pl.program_id(2)
    base = pl.multiple_of(t * DIL_TILE, DIL_TILE)
    pbase = pl.multiple_of(jnp.maximum(t - 1, 0) * DIL_TILE, DIL_TILE)
    first = t == 0

    def attend(q, kw, vw, bias, rows, size, init):
        s = lax.dot_general(q, kw, (((1,), (1,)), ((), ())), preferred_element_type=f32) + bias
        m_blk = jnp.max(s, axis=1, keepdims=True)
        if init:
            m_new = m_blk
        else:
            m_old = jnp.concatenate([m_ref[r:r + size, :] for r in rows], axis=0)
            l_old = jnp.concatenate([l_ref[r:r + size, :] for r in rows], axis=0)
            o_old = jnp.concatenate([acc_ref[r:r + size, :] for r in rows], axis=0)
            m_new = jnp.maximum(m_old, m_blk)
        p = jnp.exp2(s - m_new)
        l_new = jnp.sum(p, axis=1, keepdims=True)
        o_new = jnp.dot(p.astype(bf16), vw, preferred_element_type=f32)
        if not init:
            alpha = jnp.exp2(m_old - m_new)
            l_new = alpha * l_old + l_new
            o_new = alpha * o_old + o_new
        for c, r in enumerate(rows):
            m_ref[r:r + size, :] = m_new[c * size:(c + 1) * size]
            l_ref[r:r + size, :] = l_new[c * size:(c + 1) * size]
            acc_ref[r:r + size, :] = o_new[c * size:(c + 1) * size]

    def window(ref, rows, size, n):
        if n > 0:
            return jnp.concatenate([ref[0, pl.ds(base + (r - size), 2 * size), :] for r in rows], axis=0)
        parts = []
        for r in rows:
            parts.append(ref[0, pl.ds(pbase + (r + DIL_ROWS - size), size), :])
            parts.append(ref[0, pl.ds(base + r, size), :])
        return jnp.concatenate(parts, axis=0)

    slope = slopes_ref[2 * DIL_HEADS + h]
    bias = -slope * jnp.where(first, d16f_ref[...], d16_ref[...])
    for rho in range(DIL_RES):
        rows = [rho * DIL_ROWS]
        q = q2_ref[0, rows[0]:rows[0] + DIL_ROWS, :]
        attend(q, window(k2_ref, rows, DIL_ROWS, 0), window(v_ref, rows, DIL_ROWS, 0),
               bias, rows, DIL_ROWS, True)

    slope = slopes_ref[DIL_HEADS + h]
    bias_in = -slope * d4_ref[...]
    bias_edge = -slope * jnp.where(first, d4f_ref[...], d4_ref[...])
    for rho4 in range(4):
        for n in range(4):
            rows = [(rho4 + 4 * c) * DIL_ROWS + 32 * n for c in range(4)]
            q = jnp.concatenate([q1_ref[0, r:r + 32, :] for r in rows], axis=0)
            attend(q, window(k1_ref, rows, 32, n), window(v_ref, rows, 32, n),
                   bias_edge if n == 0 else bias_in, rows, 32, False)

    slope = slopes_ref[h]
    bias_in = -slope * d1_ref[...]
    bias_edge = -slope * jnp.where(first, d1f_ref[...], d1_ref[...])
    for n in range(DIL_ROWS // 16):
        rows = [rho * DIL_ROWS + 16 * n for rho in range(DIL_RES)]
        q = jnp.concatenate([q0_ref[0, r:r + 16, :] for r in rows], axis=0)
        attend(q, window(k0_ref, rows, 16, n), window(v_ref, rows, 16, n),
               bias_edge if n == 0 else bias_in, rows, 16, False)

    o_ref[0] = (acc_ref[...] / l_ref[...]).astype(bf16)


def dil_attention(p3, slopes):
    tables = [jnp.asarray(a) for pair in _dil_distance_tables() for a in pair]
    gh = DIL_HEADS
    kcol0 = N_GROUPS * gh
    vcol0 = 2 * N_GROUPS * gh * HEAD_DIM // DIL_V_DIM
    qspecs = [pl.BlockSpec((1, DIL_TILE, HEAD_DIM), lambda b, h, t, g=g: (b, t, g * gh + h))
              for g in range(N_GROUPS)]
    kspecs = [pl.BlockSpec((1, SEQ, HEAD_DIM), lambda b, h, t, g=g: (b, 0, kcol0 + g * gh + h))
              for g in range(N_GROUPS)]
    vspec = pl.BlockSpec((1, SEQ, DIL_V_DIM), lambda b, h, t: (b, 0, vcol0 + h))
    tspecs = [pl.BlockSpec(a.shape, lambda b, h, t: (0, 0)) for a in tables]
    return pl.pallas_call(
        _dil_attn_kernel,
        out_shape=jax.ShapeDtypeStruct((BATCH, SEQ, D_MODEL), bf16),
        grid=(BATCH, DIL_HEADS, SEQ // DIL_TILE),
        in_specs=[pl.BlockSpec(memory_space=pltpu.SMEM)] + qspecs + kspecs + [vspec] + tspecs,
        out_specs=pl.BlockSpec((1, DIL_TILE, DIL_V_DIM), lambda b, h, t: (b, t, h)),
        scratch_shapes=[pltpu.VMEM((DIL_TILE, 1), f32),
                        pltpu.VMEM((DIL_TILE, 1), f32),
                        pltpu.VMEM((DIL_TILE, DIL_V_DIM), f32)],
        compiler_params=_params(("arbitrary",) * 3, 48),
        name="dil_attention",
    )(slopes, p3, p3, p3, p3, p3, p3, p3, *tables)


def _fox_mixer(h2d, wb, b_f, q_gain, k_gain):
    h3 = h2d.reshape(BATCH, SEQ, D_MODEL)
    qkv_cols = 3 * FOX_HEADS * HEAD_DIM
    wf = jnp.pad(wb[:, qkv_cols:], ((0, 0), (0, HEAD_DIM - FOX_HEADS)))
    bfp = jnp.pad(b_f.astype(f32), (0, HEAD_DIM - FOX_HEADS)).reshape(1, HEAD_DIM)
    c_pieces = fox_gate(h3, wf, bfp)

    nb = D_MODEL // PROJ_BN
    qg = (q_gain * (QK_SCALE * LOG2E)).astype(f32).reshape(1, HEAD_DIM)
    kg = k_gain.astype(f32).reshape(1, HEAD_DIM)
    qT = fox_proj(h3, wb, qg, "q", 0)
    k = fox_proj(h3, wb, kg, "k", nb)
    vT = fox_proj(h3, wb, kg, "v", 2 * nb)
    return fox_attention(qT, c_pieces, k, vT).reshape(TOKENS, D_MODEL)


def _dil_mixer(h_rm, wb, q_gain, k_gain):
    ones = jnp.ones((DIL_COLS // PROJ_BN - DIL_QK_BLOCKS, HEAD_DIM), f32)
    gains = jnp.concatenate([q_gain.astype(f32) * (QK_SCALE * LOG2E), k_gain.astype(f32), ones], axis=0)
    p3 = dil_proj(h_rm, wb, gains.reshape(-1, 1, HEAD_DIM))
    n_heads = N_GROUPS * DIL_HEADS
    slopes = jnp.exp2(-ALIBI_MAX_EXP * jnp.arange(1, n_heads + 1, dtype=f32) / n_heads)
    return dil_attention(p3, slopes)


def kernel(x, fox_w_in, fox_b_f, fox_q_gain, fox_k_gain, fox_w_out, dil_w_in, dil_q_gain, dil_k_gain,
           dil_w_out, mix_norm_g, mlp_norm_g, mlp_w_up, mlp_w_down):
    x0 = x.reshape(TOKENS, D_MODEL)
    o = _fox_mixer(rmsnorm(x0, mix_norm_g[0]), to_bf16(fox_w_in, 0), fox_b_f[0], fox_q_gain[0], fox_k_gain[0])
    x1, h = outproj_residual(o, to_bf16(fox_w_out, 0), x0, mlp_norm_g[0], residue_major=False)
    x2, h_rm = mlp_residual(h, to_bf16(mlp_w_up, 0), to_bf16(mlp_w_down, 0), x1, g_next=mix_norm_g[1])
    o = _dil_mixer(h_rm.reshape(TOKENS, D_MODEL), to_bf16(dil_w_in, 0), dil_q_gain[0], dil_k_gain[0])
    x3, h = outproj_residual(o, to_bf16(dil_w_out, 0), x2, mlp_norm_g[1], residue_major=True)
    x4 = mlp_residual(h, to_bf16(mlp_w_up, 1), to_bf16(mlp_w_down, 1), x3)
    return x4.reshape(BATCH, SEQ, D_MODEL)
```

```python
import functools

import numpy as np
import jax
import jax.numpy as jnp
from jax import lax
from jax.experimental import pallas as pl
from jax.experimental.pallas import tpu as pltpu

f32 = jnp.float32
bf16 = jnp.bfloat16

BATCH = 2
SEQ = 8192
D_MODEL = 2048
TOKENS = BATCH * SEQ
HEAD_DIM = 128
EPS = 1e-6
D_FF = 4 * D_MODEL
FOX_HEADS = D_MODEL // HEAD_DIM
DIL_PATTERNS = ((128, 1), (512, 4), (2048, 16))
N_GROUPS = len(DIL_PATTERNS)
DIL_SPAN = 128
DIL_HEADS = D_MODEL // (2 * HEAD_DIM)
DIL_V_DIM = D_MODEL // DIL_HEADS
ALIBI_MAX_EXP = 8.0

LOG2E = 1.4426950408889634
QK_SCALE = HEAD_DIM ** -0.5
MASKED = 1e30
MIB = 1024 * 1024

DIL_TILE = 2048
DIL_RES = 16
DIL_ROWS = DIL_TILE // DIL_RES


def _params(semantics, vmem_mib):
    return pltpu.CompilerParams(dimension_semantics=semantics, vmem_limit_bytes=vmem_mib * MIB)


def _rmsnorm_kernel(x_ref, g_ref, o_ref):
    x = x_ref[...]
    ms = jnp.mean(x * x, axis=-1, keepdims=True)
    o_ref[...] = (x * lax.rsqrt(ms + EPS) * g_ref[...]).astype(o_ref.dtype)


def rmsnorm(x2d, g):
    bm = 512
    return pl.pallas_call(
        _rmsnorm_kernel,
        out_shape=jax.ShapeDtypeStruct((TOKENS, D_MODEL), bf16),
        grid=(TOKENS // bm,),
        in_specs=[pl.BlockSpec((bm, D_MODEL), lambda i: (i, 0)),
                  pl.BlockSpec((1, D_MODEL), lambda i: (0, 0))],
        out_specs=pl.BlockSpec((bm, D_MODEL), lambda i: (i, 0)),
        compiler_params=_params(("arbitrary",), 32),
        name="rmsnorm",
    )(x2d, g.reshape(1, D_MODEL))


def _cast_kernel(w_ref, o_ref):
    o_ref[...] = w_ref[...].astype(o_ref.dtype)


def to_bf16(w3, layer):
    _, rows, cols = w3.shape
    br = 256 if cols > 4096 else 1024
    return pl.pallas_call(
        _cast_kernel,
        out_shape=jax.ShapeDtypeStruct((rows, cols), bf16),
        grid=(rows // br,),
        in_specs=[pl.BlockSpec((None, br, cols), lambda i: (layer, i, 0))],
        out_specs=pl.BlockSpec((br, cols), lambda i: (i, 0)),
        compiler_params=_params(("arbitrary",), 40),
        name="to_bf16",
    )(w3)


def _head_rmsnorm(y, gain):
    ms = jnp.mean(y * y, axis=-1, keepdims=True)
    return y * lax.rsqrt(ms + EPS) * gain


GATE_BM = 512


def _gate_kernel(h_ref, wf_ref, bf_ref, hi_ref, mid_ref, lo_ref, carry_ref):
    @pl.when(pl.program_id(1) == 0)
    def _():
        carry_ref[...] = jnp.zeros_like(carry_ref)

    f = jnp.dot(h_ref[0], wf_ref[...], preferred_element_type=f32) + bf_ref[...]
    lf = jnp.minimum(f, 0.0) - jnp.log1p(jnp.exp(-jnp.abs(f)))
    row = lax.broadcasted_iota(jnp.int32, (GATE_BM, GATE_BM), 0)
    col = lax.broadcasted_iota(jnp.int32, (GATE_BM, GATE_BM), 1)
    tri = (col <= row).astype(bf16)
    p0 = lf.astype(bf16)
    r0 = lf - p0.astype(f32)
    p1 = r0.astype(bf16)
    p2 = (r0 - p1.astype(f32)).astype(bf16)
    cs = (jnp.dot(tri, p0, preferred_element_type=f32)
          + jnp.dot(tri, p1, preferred_element_type=f32)
          + jnp.dot(tri, p2, preferred_element_type=f32))
    c = cs + carry_ref[...]
    carry_ref[...] = c[GATE_BM - 1:GATE_BM, :]
    c2 = c * LOG2E
    hi = c2.astype(bf16)
    r1 = c2 - hi.astype(f32)
    mid = r1.astype(bf16)
    lo = (r1 - mid.astype(f32)).astype(bf16)
    hi_ref[0] = hi
    mid_ref[0] = mid
    lo_ref[0] = lo


def fox_gate(h3, wf, bfp):
    piece = jax.ShapeDtypeStruct((BATCH, SEQ, HEAD_DIM), bf16)
    spec = pl.BlockSpec((1, GATE_BM, HEAD_DIM), lambda b, s: (b, s, 0))
    return pl.pallas_call(
        _gate_kernel,
        out_shape=(piece, piece, piece),
        grid=(BATCH, SEQ // GATE_BM),
        in_specs=[pl.BlockSpec((1, GATE_BM, D_MODEL), lambda b, s: (b, s, 0)),
                  pl.BlockSpec((D_MODEL, HEAD_DIM), lambda b, s: (0, 0)),
                  pl.BlockSpec((1, HEAD_DIM), lambda b, s: (0, 0))],
        out_specs=(spec, spec, spec),
        scratch_shapes=[pltpu.VMEM((1, HEAD_DIM), f32)],
        compiler_params=_params(("arbitrary", "arbitrary"), 32),
        name="fox_gate",
    )(h3, wf, bfp)


PROJ_BM = 1024
PROJ_BN = 1024
PROJ_HEADS = PROJ_BN // HEAD_DIM
MXU_N = 256
FOX_BK = 256
FOX_V_ROWS = HEAD_DIM + 16


def _fox_proj_kernel(h_ref, w_ref, g_ref, o_ref, *, mode):
    h = h_ref[0]
    if mode == "v":
        extra = lax.broadcasted_iota(jnp.int32, (FOX_V_ROWS - HEAD_DIM, FOX_BK), 0)
        ones_row = jnp.where(extra == 0, 1.0, 0.0).astype(bf16)
    for s in range(PROJ_BN // MXU_N):
        y2 = jnp.dot(h, w_ref[:, s * MXU_N:(s + 1) * MXU_N], preferred_element_type=f32)
        for u in range(MXU_N // HEAD_DIM):
            hh = s * (MXU_N // HEAD_DIM) + u
            y = y2[:, u * HEAD_DIM:(u + 1) * HEAD_DIM]
            if mode == "q":
                o_ref[0, hh] = _head_rmsnorm(y, g_ref[...]).T.astype(bf16)
            elif mode == "k":
                o_ref[0, hh] = _head_rmsnorm(y, g_ref[...]).astype(bf16)
            else:
                for c in range(PROJ_BM // FOX_BK):
                    o_ref[0, hh, c, :HEAD_DIM, :] = y[c * FOX_BK:(c + 1) * FOX_BK, :].T.astype(bf16)
                    o_ref[0, hh, c, HEAD_DIM:, :] = ones_row


def fox_proj(h3, w, gain, mode, col_block0):
    nsb = SEQ // PROJ_BM
    if mode == "q":
        shape = (BATCH, FOX_HEADS, HEAD_DIM, SEQ)
        ospec = pl.BlockSpec((1, PROJ_HEADS, HEAD_DIM, PROJ_BM), lambda b, s, j: (b, j, 0, s))
    elif mode == "k":
        shape = (BATCH, FOX_HEADS, SEQ, HEAD_DIM)
        ospec = pl.BlockSpec((1, PROJ_HEADS, PROJ_BM, HEAD_DIM), lambda b, s, j: (b, j, s, 0))
    else:
        shape = (BATCH, FOX_HEADS, SEQ // FOX_BK, FOX_V_ROWS, FOX_BK)
        ospec = pl.BlockSpec((1, PROJ_HEADS, PROJ_BM // FOX_BK, FOX_V_ROWS, FOX_BK),
                             lambda b, s, j: (b, j, s, 0, 0))
    return pl.pallas_call(
        functools.partial(_fox_proj_kernel, mode=mode),
        out_shape=jax.ShapeDtypeStruct(shape, bf16),
        grid=(BATCH, nsb, D_MODEL // PROJ_BN),
        in_specs=[pl.BlockSpec((1, PROJ_BM, D_MODEL), lambda b, s, j: (b, s, 0)),
                  pl.BlockSpec((D_MODEL, PROJ_BN), lambda b, s, j: (0, col_block0 + j)),
                  pl.BlockSpec((1, HEAD_DIM), lambda b, s, j: (0, 0))],
        out_specs=ospec,
        compiler_params=_params(("arbitrary",) * 3, 48),
        name="fox_proj_" + mode,
    )(h3, w, gain)


FOX_BQ = 1024
FOX_TILES = FOX_BQ // FOX_BK
FOX_LEAD = 4
AUG = 2 * HEAD_DIM
AUG_ROWS = 16
AUG_CHUNK = 1024


def _fox_attn_kernel(qT_ref, chi_ref, cmid_ref, clo_ref, k_ref, vT_ref, o_ref,
                     kaug_ref, s_ref, m_ref, acc_ref):
    h = pl.program_id(1)
    qi = pl.program_id(2)
    pieces = (chi_ref, cmid_ref, clo_ref)
    n_p = len(pieces)

    @pl.when(qi == 0)
    def _():
        kaug_ref[:, :HEAD_DIM] = k_ref[0, 0]
        row = lax.broadcasted_iota(jnp.int32, (HEAD_DIM, HEAD_DIM), 0)
        col = lax.broadcasted_iota(jnp.int32, (HEAD_DIM, HEAD_DIM), 1)
        lane = lax.broadcasted_iota(jnp.int32, (AUG_CHUNK, HEAD_DIM), 1)
        for c in range(SEQ // AUG_CHUNK):
            rows = slice(c * AUG_CHUNK, (c + 1) * AUG_CHUNK)
            aug = jnp.where(lane < n_p, 1.0, 0.0)
            for p, ref in enumerate(pieces):
                sel = jnp.where((row == h) & (col == n_p + p), -1.0, 0.0).astype(bf16)
                aug = aug + jnp.dot(ref[0, rows, :], sel, preferred_element_type=f32)
            kaug_ref[rows, HEAD_DIM:] = aug.astype(bf16)

    q0 = pl.multiple_of(qi * FOX_BQ, FOX_BQ)
    row = lax.broadcasted_iota(jnp.int32, (AUG_ROWS, HEAD_DIM), 0)
    col = lax.broadcasted_iota(jnp.int32, (AUG_ROWS, HEAD_DIM), 1)
    slot = lax.broadcasted_iota(jnp.int32, (AUG_ROWS, FOX_BQ), 0)
    qa = jnp.where((slot >= n_p) & (slot < 2 * n_p), 1.0, 0.0)
    for p, ref in enumerate(pieces):
        sel = jnp.where((row == p) & (col == h), 1.0, 0.0).astype(bf16)
        qa = qa + lax.dot_general(sel, ref[0, pl.ds(q0, FOX_BQ), :], (((1,), (1,)), ((), ())),
                                  preferred_element_type=f32)
    q_aug = jnp.concatenate([qT_ref[0, 0], qa.astype(bf16),
                             jnp.zeros((HEAD_DIM - AUG_ROWS, FOX_BQ), bf16)], axis=0)

    def scores(kb, q):
        start = pl.multiple_of(kb * FOX_BK, FOX_BK)
        return jnp.dot(kaug_ref[pl.ds(start, FOX_BK), :], q, preferred_element_type=f32)

    def softmax_step(s, m):
        m_new = jnp.maximum(m, jnp.max(s, axis=0, keepdims=True))
        return m_new, jnp.exp2(m - m_new), jnp.exp2(s - m_new).astype(bf16)

    def pv(kb, p):
        return jnp.dot(vT_ref[0, 0, kb], p, preferred_element_type=f32)

    n_slabs = FOX_BQ // MXU_N
    slabs = [slice(n * MXU_N, (n + 1) * MXU_N) for n in range(n_slabs)]
    q_slabs = [q_aug[:, c] for c in slabs]
    kk = lax.broadcasted_iota(jnp.int32, (FOX_BK, MXU_N), 0)
    qq = lax.broadcasted_iota(jnp.int32, (FOX_BK, MXU_N), 1)

    def run_units(units, lookahead):
        m = [m_ref[:, c] for c in slabs]
        acc = [acc_ref[:, c] for c in slabs]
        todo = units + lookahead
        s_val = {i: s_ref[:, slabs[i]] for i in range(FOX_LEAD)}
        soft = {}
        for x in range(len(units) + 1):
            if x + FOX_LEAD < len(todo):
                kb, n, _ = todo[x + FOX_LEAD]
                s_val[x + FOX_LEAD] = scores(kb, q_slabs[n])
            if x >= 1:
                kb, n, _ = units[x - 1]
                alpha, p = soft.pop(x - 1)
                acc[n] = alpha * acc[n] + pv(kb, p)
            if x < len(units):
                kb, n, diagonal = units[x]
                s = s_val.pop(x)
                if diagonal:
                    s = jnp.where(kk <= qq, s, -MASKED)
                m[n], alpha, p = softmax_step(s, m[n])
                soft[x] = (alpha, p)
        for i in range(len(lookahead)):
            s_ref[:, slabs[i]] = s_val[len(units) + i]
        for n, c in enumerate(slabs):
            m_ref[:, c], acc_ref[:, c] = m[n], acc[n]

    for i in range(FOX_LEAD):
        s_ref[:, slabs[i]] = scores(0, q_slabs[i])
    m_ref[...] = jnp.full_like(m_ref, -MASKED)
    acc_ref[...] = jnp.zeros_like(acc_ref)

    def body(j, carry):
        kb0 = j * FOX_TILES
        run_units([(kb0 + u, n, False) for u in range(FOX_TILES) for n in range(n_slabs)],
                  [(kb0 + FOX_TILES, n, False) for n in range(FOX_LEAD)])
        return carry

    lax.fori_loop(0, qi, body, 0)

    kb0 = qi * FOX_TILES
    run_units([(kb0 + d, n, n == d) for d in range(FOX_TILES) for n in range(d, n_slabs)], [])
    o = acc_ref[:HEAD_DIM, :] / acc_ref[HEAD_DIM:HEAD_DIM + 1, :]
    o_ref[0] = o.T.astype(bf16)


def fox_attention(qT, c_pieces, k, vT):
    nq = SEQ // FOX_BQ
    nk = SEQ // FOX_BK
    cspec = pl.BlockSpec((1, SEQ, HEAD_DIM), lambda b, h, i: (b, 0, 0))
    return pl.pallas_call(
        _fox_attn_kernel,
        out_shape=jax.ShapeDtypeStruct((BATCH, SEQ, D_MODEL), bf16),
        grid=(BATCH, FOX_HEADS, nq),
        in_specs=[pl.BlockSpec((1, 1, HEAD_DIM, FOX_BQ), lambda b, h, i: (b, h, 0, i)),
                  cspec, cspec, cspec,
                  pl.BlockSpec((1, 1, SEQ, HEAD_DIM), lambda b, h, i: (b, h, 0, 0)),
                  pl.BlockSpec((1, 1, nk, FOX_V_ROWS, FOX_BK), lambda b, h, i: (b, h, 0, 0, 0))],
        out_specs=pl.BlockSpec((1, FOX_BQ, HEAD_DIM), lambda b, h, i: (b, i, h)),
        scratch_shapes=[pltpu.VMEM((SEQ, AUG), bf16),
                        pltpu.VMEM((FOX_BK, FOX_LEAD * MXU_N), f32),
                        pltpu.VMEM((1, FOX_BQ), f32),
                        pltpu.VMEM((FOX_V_ROWS, FOX_BQ), f32)],
        compiler_params=_params(("arbitrary",) * 3, 48),
        name="fox_attention",
    )(qT, *c_pieces, k, vT)


OUT_BM = 512
OUT_SLAB = 512
LANES = 128
DIL_QUARTER = OUT_BM // DIL_RES


def _rmsnorm_rows(y, g):
    ms = jnp.mean(y * y, axis=-1, keepdims=True)
    return y * lax.rsqrt(ms + EPS) * g


def _outproj_kernel(o_ref, w_ref, x_ref, g_ref, y_ref, h_ref):
    for s in range(D_MODEL // OUT_SLAB):
        cols = slice(s * OUT_SLAB, (s + 1) * OUT_SLAB)
        y_ref[:, cols] = x_ref[:, cols] + jnp.dot(o_ref[...], w_ref[:, cols], preferred_element_type=f32)
    h_ref[...] = _rmsnorm_rows(y_ref[...], g_ref[...]).astype(bf16)


def _dil_outproj_kernel(o_ref, w_ref, x_ref, g_ref, y_ref, h_ref, nat_ref):
    o = o_ref[0].reshape(OUT_BM, D_MODEL)
    per = MXU_N // LANES
    for s in range(D_MODEL // MXU_N):
        mix = jnp.dot(o, w_ref[:, s * MXU_N:(s + 1) * MXU_N], preferred_element_type=f32)
        for u in range(per):
            c = s * per + u
            for rho in range(DIL_RES):
                nat_ref[c, pl.ds(rho, DIL_QUARTER, stride=DIL_RES), :] = (
                    mix[rho * DIL_QUARTER:(rho + 1) * DIL_QUARTER, u * LANES:(u + 1) * LANES])
            cols = slice(c * LANES, (c + 1) * LANES)
            y_ref[:, cols] = x_ref[:, cols] + nat_ref[c]
    h_ref[...] = _rmsnorm_rows(y_ref[...], g_ref[...]).astype(bf16)


def outproj_residual(o, w, x2d, g, residue_major):
    row = pl.BlockSpec((OUT_BM, D_MODEL), lambda i: (i, 0))
    if residue_major:
        quarters = DIL_TILE // OUT_BM
        o = o.reshape(TOKENS // DIL_TILE, DIL_RES, DIL_ROWS, D_MODEL)
        ospec = pl.BlockSpec((1, DIL_RES, DIL_QUARTER, D_MODEL), lambda i: (i // quarters, 0, i % quarters, 0))
        body, scratch = _dil_outproj_kernel, [pltpu.VMEM((D_MODEL // LANES, OUT_BM, LANES), f32)]
    else:
        ospec, body, scratch = row, _outproj_kernel, []
    return pl.pallas_call(
        body,
        out_shape=(jax.ShapeDtypeStruct((TOKENS, D_MODEL), f32), jax.ShapeDtypeStruct((TOKENS, D_MODEL), bf16)),
        grid=(TOKENS // OUT_BM,),
        in_specs=[ospec,
                  pl.BlockSpec((D_MODEL, D_MODEL), lambda i: (0, 0)),
                  row,
                  pl.BlockSpec((1, D_MODEL), lambda i: (0, 0))],
        out_specs=(row, row),
        scratch_shapes=scratch,
        compiler_params=_params(("arbitrary",), 52),
        name="dil_outproj_residual" if residue_major else "outproj_residual",
    )(o, w, x2d, g.reshape(1, D_MODEL))


MLP_BM = OUT_BM
MLP_BF = 1024


def _mlp_kernel(h_ref, wu_ref, wd_ref, x_ref, *refs, next_h):
    if next_h is None:
        (y_ref,) = refs
    else:
        g_ref, y_ref, hn_ref, slab_ref = refs
    f = pl.program_id(1)

    @pl.when(f == 0)
    def _():
        y_ref[...] = x_ref[...]

    a = jnp.maximum(jnp.dot(h_ref[...], wu_ref[...], preferred_element_type=f32), 0.0)
    y_ref[...] += jnp.dot((a * a).astype(bf16), wd_ref[...], preferred_element_type=f32)

    if next_h is not None:
        @pl.when(f == pl.num_programs(1) - 1)
        def _():
            hn = _rmsnorm_rows(y_ref[...], g_ref[...])
            for c in range(D_MODEL // LANES):
                slab_ref[c] = hn[:, c * LANES:(c + 1) * LANES]
            for rho in range(DIL_RES):
                for c in range(D_MODEL // LANES):
                    hn_ref[0, rho, :, c * LANES:(c + 1) * LANES] = (
                        slab_ref[c, pl.ds(rho, DIL_QUARTER, stride=DIL_RES), :].astype(bf16))


def mlp_residual(h2d, w_up, w_down, x2d, g_next=None):
    row = pl.BlockSpec((MLP_BM, D_MODEL), lambda i, f: (i, 0))
    in_specs = [row,
                pl.BlockSpec((D_MODEL, MLP_BF), lambda i, f: (0, f)),
                pl.BlockSpec((MLP_BF, D_MODEL), lambda i, f: (f, 0)),
                row]
    y_shape = jax.ShapeDtypeStruct((TOKENS, D_MODEL), f32)
    if g_next is None:
        args, out_shape, out_specs, scratch = (h2d, w_up, w_down, x2d), y_shape, row, []
    else:
        quarters = DIL_TILE // MLP_BM
        in_specs.append(pl.BlockSpec((1, D_MODEL), lambda i, f: (0, 0)))
        args = (h2d, w_up, w_down, x2d, g_next.reshape(1, D_MODEL))
        out_shape = (y_shape, jax.ShapeDtypeStruct((TOKENS // DIL_TILE, DIL_RES, DIL_ROWS, D_MODEL), bf16))
        out_specs = (row, pl.BlockSpec((1, DIL_RES, DIL_QUARTER, D_MODEL),
                                       lambda i, f: (i // quarters, 0, i % quarters, 0)))
        scratch = [pltpu.VMEM((D_MODEL // LANES, MLP_BM, LANES), f32)]
    return pl.pallas_call(
        functools.partial(_mlp_kernel, next_h=None if g_next is None else "residue_major"),
        out_shape=out_shape,
        grid=(TOKENS // MLP_BM, D_FF // MLP_BF),
        in_specs=in_specs,
        out_specs=out_specs,
        scratch_shapes=scratch,
        compiler_params=_params(("arbitrary", "arbitrary"), 56),
        name="mlp_residual",
    )(*args)


DIL_COLS = 2 * N_GROUPS * DIL_HEADS * HEAD_DIM + DIL_HEADS * DIL_V_DIM
DIL_QK_BLOCKS = 2 * N_GROUPS


def _dil_proj_kernel(h_ref, w_ref, g_ref, o_ref):
    j = pl.program_id(1)
    h = h_ref[...]

    def slabs(normalise):
        for s in range(PROJ_BN // MXU_N):
            y2 = jnp.dot(h, w_ref[:, s * MXU_N:(s + 1) * MXU_N], preferred_element_type=f32)
            if normalise:
                for u in range(MXU_N // HEAD_DIM):
                    cols = slice(s * MXU_N + u * HEAD_DIM, s * MXU_N + (u + 1) * HEAD_DIM)
                    o_ref[:, cols] = _head_rmsnorm(y2[:, u * HEAD_DIM:(u + 1) * HEAD_DIM], g_ref[0]).astype(bf16)
            else:
                o_ref[:, s * MXU_N:(s + 1) * MXU_N] = y2.astype(bf16)

    @pl.when(j < DIL_QK_BLOCKS)
    def _():
        slabs(True)

    @pl.when(j >= DIL_QK_BLOCKS)
    def _():
        slabs(False)


def dil_proj(h2d, w, gains):
    out = pl.pallas_call(
        _dil_proj_kernel,
        out_shape=jax.ShapeDtypeStruct((TOKENS, DIL_COLS), bf16),
        grid=(TOKENS // PROJ_BM, DIL_COLS // PROJ_BN),
        in_specs=[pl.BlockSpec((PROJ_BM, D_MODEL), lambda i, j: (i, 0)),
                  pl.BlockSpec((D_MODEL, PROJ_BN), lambda i, j: (0, j)),
                  pl.BlockSpec((1, 1, HEAD_DIM), lambda i, j: (j, 0, 0))],
        out_specs=pl.BlockSpec((PROJ_BM, PROJ_BN), lambda i, j: (i, j)),
        compiler_params=_params(("arbitrary", "arbitrary"), 48),
        name="dil_proj",
    )(h2d, w, gains)
    return out.reshape(BATCH, SEQ, DIL_COLS)


def _dil_distance_tables():
    def table(delta, prev_half, r):
        valid = (delta >= 0) & (delta <= DIL_SPAN)
        d = np.where(valid, delta * (r * LOG2E), MASKED).astype(np.float32)
        first = np.where(prev_half, np.float32(MASKED), d).astype(np.float32)
        return d, first

    i = np.arange(128)[:, None]
    j = np.arange(256)[None, :]
    t16 = table(128 + i - j, j < 128, 16)
    c, u = i // 32, i % 32
    cp, half, up = j // 64, (j % 64) // 32, j % 32
    t4 = table(128 + 4 * u + c - 128 * half - 4 * up - cp, half == 0, 4)
    i = np.arange(256)[:, None]
    j = np.arange(512)[None, :]
    rho, u = i // 16, i % 16
    rp, half, up = j // 32, (j % 32) // 16, j % 16
    t1 = table(256 + 16 * u + rho - 256 * half - 16 * up - rp, half == 0, 1)
    return t1, t4, t16


def _dil_attn_kernel(slopes_ref, q0_ref, q1_ref, q2_ref, k0_ref, k1_ref, k2_ref, v_ref,
                     d1_ref, d1f_ref, d4_ref, d4f_ref, d16_ref, d16f_ref,
                     o_ref, m_ref, l_ref, acc_ref):
    h = pl.program_id(1)
    t = pl.program_id(2)
    base = pl.multiple_of(t * DIL_TILE, DIL_TILE)
    pbase = pl.multiple_of(jnp.maximum(t - 1, 0) * DIL_TILE, DIL_TILE)
    first = t == 0

    def attend(q, kw, vw, bias, rows, size, init):
        s = lax.dot_general(q, kw, (((1,), (1,)), ((), ())), preferred_element_type=f32) + bias
        m_blk = jnp.max(s, axis=1, keepdims=True)
        if init:
            m_new = m_blk
        else:
            m_old = jnp.concatenate([m_ref[r:r + size, :] for r in rows], axis=0)
            l_old = jnp.concatenate([l_ref[r:r + size, :] for r in rows], axis=0)
            o_old = jnp.concatenate([acc_ref[r:r + size, :] for r in rows], axis=0)
            m_new = jnp.maximum(m_old, m_blk)
        p = jnp.exp2(s - m_new)
        l_new = jnp.sum(p, axis=1, keepdims=True)
        o_new = jnp.dot(p.astype(bf16), vw, preferred_element_type=f32)
        if not init:
            alpha = jnp.exp2(m_old - m_new)
            l_new = alpha * l_old + l_new
            o_new = alpha * o_old + o_new
        for c, r in enumerate(rows):
            m_ref[r:r + size, :] = m_new[c * size:(c + 1) * size]
            l_ref[r:r + size, :] = l_new[c * size:(c + 1) * size]
            acc_ref[r:r + size, :] = o_new[c * size:(c + 1) * size]

    def window(ref, rows, size, n):
        if n > 0:
            return jnp.concatenate([ref[0, pl.ds(base + (r - size), 2 * size), :] for r in rows], axis=0)
        parts = []
        for r in rows:
            parts.append(ref[0, pl.ds(pbase + (r + DIL_ROWS - size), size), :])
            parts.append(ref[0, pl.ds(base + r, size), :])
        return jnp.concatenate(parts, axis=0)

    slope = slopes_ref[2 * DIL_HEADS + h]
    bias = -slope * jnp.where(first, d16f_ref[...], d16_ref[...])
    for rho in range(DIL_RES):
        rows = [rho * DIL_ROWS]
        q = q2_ref[0, rows[0]:rows[0] + DIL_ROWS, :]
        attend(q, window(k2_ref, rows, DIL_ROWS, 0), window(v_ref, rows, DIL_ROWS, 0),
               bias, rows, DIL_ROWS, True)

    slope = slopes_ref[DIL_HEADS + h]
    bias_in = -slope * d4_ref[...]
    bias_edge = -slope * jnp.where(first, d4f_ref[...], d4_ref[...])
    for rho4 in range(4):
        for n in range(4):
            rows = [(rho4 + 4 * c) * DIL_ROWS + 32 * n for c in range(4)]
            q = jnp.concatenate([q1_ref[0, r:r + 32, :] for r in rows], axis=0)
            attend(q, window(k1_ref, rows, 32, n), window(v_ref, rows, 32, n),
                   bias_edge if n == 0 else bias_in, rows, 32, False)

    slope = slopes_ref[h]
    bias_in = -slope * d1_ref[...]
    bias_edge = -slope * jnp.where(first, d1f_ref[...], d1_ref[...])
    for n in range(DIL_ROWS // 16):
        rows = [rho * DIL_ROWS + 16 * n for rho in range(DIL_RES)]
        q = jnp.concatenate([q0_ref[0, r:r + 16, :] for r in rows], axis=0)
        attend(q, window(k0_ref, rows, 16, n), window(v_ref, rows, 16, n),
               bias_edge if n == 0 else bias_in, rows, 16, False)

    o_ref[0] = (acc_ref[...] / l_ref[...]).astype(bf16)


def dil_attention(p3, slopes):
    tables = [jnp.asarray(a) for pair in _dil_distance_tables() for a in pair]
    gh = DIL_HEADS
    kcol0 = N_GROUPS * gh
    vcol0 = 2 * N_GROUPS * gh * HEAD_DIM // DIL_V_DIM
    qspecs = [pl.BlockSpec((1, DIL_TILE, HEAD_DIM), lambda b, h, t, g=g: (b, t, g * gh + h))
              for g in range(N_GROUPS)]
    kspecs = [pl.BlockSpec((1, SEQ, HEAD_DIM), lambda b, h, t, g=g: (b, 0, kcol0 + g * gh + h))
              for g in range(N_GROUPS)]
    vspec = pl.BlockSpec((1, SEQ, DIL_V_DIM), lambda b, h, t: (b, 0, vcol0 + h))
    tspecs = [pl.BlockSpec(a.shape, lambda b, h, t: (0, 0)) for a in tables]
    return pl.pallas_call(
        _dil_attn_kernel,
        out_shape=jax.ShapeDtypeStruct((BATCH, SEQ, D_MODEL), bf16),
        grid=(BATCH, DIL_HEADS, SEQ // DIL_TILE),
        in_specs=[pl.BlockSpec(memory_space=pltpu.SMEM)] + qspecs + kspecs + [vspec] + tspecs,
        out_specs=pl.BlockSpec((1, DIL_TILE, DIL_V_DIM), lambda b, h, t: (b, t, h)),
        scratch_shapes=[pltpu.VMEM((DIL_TILE, 1), f32),
                        pltpu.VMEM((DIL_TILE, 1), f32),
                        pltpu.VMEM((DIL_TILE, DIL_V_DIM), f32)],
        compiler_params=_params(("arbitrary",) * 3, 48),
        name="dil_attention",
    )(slopes, p3, p3, p3, p3, p3, p3, p3, *tables)


def _fox_mixer(h2d, wb, b_f, q_gain, k_gain):
    h3 = h2d.reshape(BATCH, SEQ, D_MODEL)
    qkv_cols = 3 * FOX_HEADS * HEAD_DIM
    wf = jnp.pad(wb[:, qkv_cols:], ((0, 0), (0, HEAD_DIM - FOX_HEADS)))
    bfp = jnp.pad(b_f.astype(f32), (0, HEAD_DIM - FOX_HEADS)).reshape(1, HEAD_DIM)
    c_pieces = fox_gate(h3, wf, bfp)

    nb = D_MODEL // PROJ_BN
    qg = (q_gain * (QK_SCALE * LOG2E)).astype(f32).reshape(1, HEAD_DIM)
    kg = k_gain.astype(f32).reshape(1, HEAD_DIM)
    qT = fox_proj(h3, wb, qg, "q", 0)
    k = fox_proj(h3, wb, kg, "k", nb)
    vT = fox_proj(h3, wb, kg, "v", 2 * nb)
    return fox_attention(qT, c_pieces, k, vT).reshape(TOKENS, D_MODEL)


def _dil_mixer(h_rm, wb, q_gain, k_gain):
    ones = jnp.ones((DIL_COLS // PROJ_BN - DIL_QK_BLOCKS, HEAD_DIM), f32)
    gains = jnp.concatenate([q_gain.astype(f32) * (QK_SCALE * LOG2E), k_gain.astype(f32), ones], axis=0)
    p3 = dil_proj(h_rm, wb, gains.reshape(-1, 1, HEAD_DIM))
    n_heads = N_GROUPS * DIL_HEADS
    slopes = jnp.exp2(-ALIBI_MAX_EXP * jnp.arange(1, n_heads + 1, dtype=f32) / n_heads)
    return dil_attention(p3, slopes)


def kernel(x, fox_w_in, fox_b_f, fox_q_gain, fox_k_gain, fox_w_out, dil_w_in, dil_q_gain, dil_k_gain,
           dil_w_out, mix_norm_g, mlp_norm_g, mlp_w_up, mlp_w_down):
    x0 = x.reshape(TOKENS, D_MODEL)
    o = _fox_mixer(rmsnorm(x0, mix_norm_g[0]), to_bf16(fox_w_in, 0), fox_b_f[0], fox_q_gain[0], fox_k_gain[0])
    x1, h = outproj_residual(o, to_bf16(fox_w_out, 0), x0, mlp_norm_g[0], residue_major=False)
    x2, h_rm = mlp_residual(h, to_bf16(mlp_w_up, 0), to_bf16(mlp_w_down, 0), x1, g_next=mix_norm_g[1])
    o = _dil_mixer(h_rm.reshape(TOKENS, D_MODEL), to_bf16(dil_w_in, 0), dil_q_gain[0], dil_k_gain[0])
    x3, h = outproj_residual(o, to_bf16(dil_w_out, 0), x2, mlp_norm_g[1], residue_major=True)
    x4 = mlp_residual(h, to_bf16(mlp_w_up, 1), to_bf16(mlp_w_down, 1), x3)
    return x4.reshape(BATCH, SEQ, D_MODEL)
```

```python
import functools

import numpy as np
import jax
import jax.numpy as jnp
from jax import lax
from jax.experimental import pallas as pl
from jax.experimental.pallas import tpu as pltpu

f32 = jnp.float32
bf16 = jnp.bfloat16

BATCH = 2
SEQ = 8192
D_MODEL = 2048
TOKENS = BATCH * SEQ
HEAD_DIM = 128
EPS = 1e-6
D_FF = 4 * D_MODEL
FOX_HEADS = D_MODEL // HEAD_DIM
DIL_PATTERNS = ((128, 1), (512, 4), (2048, 16))
N_GROUPS = len(DIL_PATTERNS)
DIL_SPAN = 128
DIL_HEADS = D_MODEL // (2 * HEAD_DIM)
DIL_V_DIM = D_MODEL // DIL_HEADS
ALIBI_MAX_EXP = 8.0

LOG2E = 1.4426950408889634
QK_SCALE = HEAD_DIM ** -0.5
MASKED = 1e30
MIB = 1024 * 1024

DIL_TILE = 2048
DIL_RES = 16
DIL_ROWS = DIL_TILE // DIL_RES


def _params(semantics, vmem_mib):
    return pltpu.CompilerParams(dimension_semantics=semantics, vmem_limit_bytes=vmem_mib * MIB)


def _rmsnorm_kernel(x_ref, g_ref, o_ref):
    x = x_ref[...]
    ms = jnp.mean(x * x, axis=-1, keepdims=True)
    o_ref[...] = (x * lax.rsqrt(ms + EPS) * g_ref[...]).astype(o_ref.dtype)


def rmsnorm(x2d, g):
    bm = 512
    return pl.pallas_call(
        _rmsnorm_kernel,
        out_shape=jax.ShapeDtypeStruct((TOKENS, D_MODEL), bf16),
        grid=(TOKENS // bm,),
        in_specs=[pl.BlockSpec((bm, D_MODEL), lambda i: (i, 0)),
                  pl.BlockSpec((1, D_MODEL), lambda i: (0, 0))],
        out_specs=pl.BlockSpec((bm, D_MODEL), lambda i: (i, 0)),
        compiler_params=_params(("arbitrary",), 32),
        name="rmsnorm",
    )(x2d, g.reshape(1, D_MODEL))


def _cast_kernel(w_ref, o_ref):
    o_ref[...] = w_ref[...].astype(o_ref.dtype)


def to_bf16(w3, layer):
    _, rows, cols = w3.shape
    br = 256 if cols > 4096 else 1024
    return pl.pallas_call(
        _cast_kernel,
        out_shape=jax.ShapeDtypeStruct((rows, cols), bf16),
        grid=(rows // br,),
        in_specs=[pl.BlockSpec((None, br, cols), lambda i: (layer, i, 0))],
        out_specs=pl.BlockSpec((br, cols), lambda i: (i, 0)),
        compiler_params=_params(("arbitrary",), 40),
        name="to_bf16",
    )(w3)


def _cast_split_kernel(w_ref, o_ref, t_ref):
    split = o_ref.shape[1]
    tail = w_ref.shape[1] - split
    o_ref[...] = w_ref[:, :split].astype(o_ref.dtype)
    t_ref[...] = jnp.zeros_like(t_ref)
    t_ref[:, :tail] = w_ref[:, split:].astype(t_ref.dtype)


def to_bf16_split(w3, layer, split):
    _, rows, cols = w3.shape
    br = 256
    return pl.pallas_call(
        _cast_split_kernel,
        out_shape=(jax.ShapeDtypeStruct((rows, split), bf16), jax.ShapeDtypeStruct((rows, HEAD_DIM), bf16)),
        grid=(rows // br,),
        in_specs=[pl.BlockSpec((None, br, cols), lambda i: (layer, i, 0))],
        out_specs=(pl.BlockSpec((br, split), lambda i: (i, 0)), pl.BlockSpec((br, HEAD_DIM), lambda i: (i, 0))),
        compiler_params=_params(("arbitrary",), 40),
        name="to_bf16_split",
    )(w3)


def _head_rmsnorm(y, gain):
    ms = jnp.mean(y * y, axis=-1, keepdims=True)
    return y * lax.rsqrt(ms + EPS) * gain


GATE_BM = 512


def _gate_kernel(h_ref, wf_ref, bf_ref, hi_ref, mid_ref, lo_ref, carry_ref):
    @pl.when(pl.program_id(1) == 0)
    def _():
        carry_ref[...] = jnp.zeros_like(carry_ref)

    f = jnp.dot(h_ref[0], wf_ref[...], preferred_element_type=f32) + bf_ref[...]
    lf = jnp.minimum(f, 0.0) - jnp.log1p(jnp.exp(-jnp.abs(f)))
    row = lax.broadcasted_iota(jnp.int32, (GATE_BM, GATE_BM), 0)
    col = lax.broadcasted_iota(jnp.int32, (GATE_BM, GATE_BM), 1)
    tri = (col <= row).astype(bf16)
    p0 = lf.astype(bf16)
    r0 = lf - p0.astype(f32)
    p1 = r0.astype(bf16)
    p2 = (r0 - p1.astype(f32)).astype(bf16)
    cs = (jnp.dot(tri, p0, preferred_element_type=f32)
          + jnp.dot(tri, p1, preferred_element_type=f32)
          + jnp.dot(tri, p2, preferred_element_type=f32))
    c = cs + carry_ref[...]
    carry_ref[...] = c[GATE_BM - 1:GATE_BM, :]
    c2 = c * LOG2E
    hi = c2.astype(bf16)
    r1 = c2 - hi.astype(f32)
    mid = r1.astype(bf16)
    lo = (r1 - mid.astype(f32)).astype(bf16)
    hi_ref[0] = hi
    mid_ref[0] = mid
    lo_ref[0] = lo


def fox_gate(h3, wf, bfp):
    piece = jax.ShapeDtypeStruct((BATCH, SEQ, HEAD_DIM), bf16)
    spec = pl.BlockSpec((1, GATE_BM, HEAD_DIM), lambda b, s: (b, s, 0))
    return pl.pallas_call(
        _gate_kernel,
        out_shape=(piece, piece, piece),
        grid=(BATCH, SEQ // GATE_BM),
        in_specs=[pl.BlockSpec((1, GATE_BM, D_MODEL), lambda b, s: (b, s, 0)),
                  pl.BlockSpec((D_MODEL, HEAD_DIM), lambda b, s: (0, 0)),
                  pl.BlockSpec((1, HEAD_DIM), lambda b, s: (0, 0))],
        out_specs=(spec, spec, spec),
        scratch_shapes=[pltpu.VMEM((1, HEAD_DIM), f32)],
        compiler_params=_params(("arbitrary", "arbitrary"), 32),
        name="fox_gate",
    )(h3, wf, bfp)


PROJ_BM = 1024
PROJ_BN = 1024
PROJ_HEADS = PROJ_BN // HEAD_DIM
MXU_N = 256
FOX_BK = 256
FOX_V_ROWS = HEAD_DIM + 16


def _fox_proj_kernel(h_ref, w_ref, g_ref, o_ref, *, mode):
    h = h_ref[0]
    if mode == "v":
        extra = lax.broadcasted_iota(jnp.int32, (FOX_V_ROWS - HEAD_DIM, FOX_BK), 0)
        ones_row = jnp.where(extra == 0, 1.0, 0.0).astype(bf16)
    for s in range(PROJ_BN // MXU_N):
        y2 = jnp.dot(h, w_ref[:, s * MXU_N:(s + 1) * MXU_N], preferred_element_type=f32)
        for u in range(MXU_N // HEAD_DIM):
            hh = s * (MXU_N // HEAD_DIM) + u
            y = y2[:, u * HEAD_DIM:(u + 1) * HEAD_DIM]
            if mode == "q":
                o_ref[0, hh] = _head_rmsnorm(y, g_ref[...]).T.astype(bf16)
            elif mode == "k":
                o_ref[0, hh] = _head_rmsnorm(y, g_ref[...]).astype(bf16)
            else:
                for c in range(PROJ_BM // FOX_BK):
                    o_ref[0, hh, c, :HEAD_DIM, :] = y[c * FOX_BK:(c + 1) * FOX_BK, :].T.astype(bf16)
                    o_ref[0, hh, c, HEAD_DIM:, :] = ones_row


def fox_proj(h3, w, gain, mode, col_block0):
    nsb = SEQ // PROJ_BM
    if mode == "q":
        shape = (BATCH, FOX_HEADS, HEAD_DIM, SEQ)
        ospec = pl.BlockSpec((1, PROJ_HEADS, HEAD_DIM, PROJ_BM), lambda b, s, j: (b, j, 0, s))
    elif mode == "k":
        shape = (BATCH, FOX_HEADS, SEQ, HEAD_DIM)
        ospec = pl.BlockSpec((1, PROJ_HEADS, PROJ_BM, HEAD_DIM), lambda b, s, j: (b, j, s, 0))
    else:
        shape = (BATCH, FOX_HEADS, SEQ // FOX_BK, FOX_V_ROWS, FOX_BK)
        ospec = pl.BlockSpec((1, PROJ_HEADS, PROJ_BM // FOX_BK, FOX_V_ROWS, FOX_BK),
                             lambda b, s, j: (b, j, s, 0, 0))
    return pl.pallas_call(
        functools.partial(_fox_proj_kernel, mode=mode),
        out_shape=jax.ShapeDtypeStruct(shape, bf16),
        grid=(BATCH, nsb, D_MODEL // PROJ_BN),
        in_specs=[pl.BlockSpec((1, PROJ_BM, D_MODEL), lambda b, s, j: (b, s, 0)),
                  pl.BlockSpec((D_MODEL, PROJ_BN), lambda b, s, j: (0, col_block0 + j)),
                  pl.BlockSpec((1, HEAD_DIM), lambda b, s, j: (0, 0))],
        out_specs=ospec,
        compiler_params=_params(("arbitrary",) * 3, 48),
        name="fox_proj_" + mode,
    )(h3, w, gain)


FOX_BQ = 1024
FOX_TILES = FOX_BQ // FOX_BK
FOX_LEAD = 4
AUG = 2 * HEAD_DIM
AUG_ROWS = 16
AUG_CHUNK = 1024


def _fox_attn_kernel(qT_ref, chi_ref, cmid_ref, clo_ref, k_ref, vT_ref, o_ref,
                     kaug_ref, s_ref, m_ref, acc_ref):
    h = pl.program_id(1)
    qi = pl.program_id(2)
    pieces = (chi_ref, cmid_ref, clo_ref)
    n_p = len(pieces)

    @pl.when(qi == 0)
    def _():
        kaug_ref[:, :HEAD_DIM] = k_ref[0, 0]
        row = lax.broadcasted_iota(jnp.int32, (HEAD_DIM, HEAD_DIM), 0)
        col = lax.broadcasted_iota(jnp.int32, (HEAD_DIM, HEAD_DIM), 1)
        lane = lax.broadcasted_iota(jnp.int32, (AUG_CHUNK, HEAD_DIM), 1)
        for c in range(SEQ // AUG_CHUNK):
            rows = slice(c * AUG_CHUNK, (c + 1) * AUG_CHUNK)
            aug = jnp.where(lane < n_p, 1.0, 0.0)
            for p, ref in enumerate(pieces):
                sel = jnp.where((row == h) & (col == n_p + p), -1.0, 0.0).astype(bf16)
                aug = aug + jnp.dot(ref[0, rows, :], sel, preferred_element_type=f32)
            kaug_ref[rows, HEAD_DIM:] = aug.astype(bf16)

    q0 = pl.multiple_of(qi * FOX_BQ, FOX_BQ)
    row = lax.broadcasted_iota(jnp.int32, (AUG_ROWS, HEAD_DIM), 0)
    col = lax.broadcasted_iota(jnp.int32, (AUG_ROWS, HEAD_DIM), 1)
    slot = lax.broadcasted_iota(jnp.int32, (AUG_ROWS, FOX_BQ), 0)
    qa = jnp.where((slot >= n_p) & (slot < 2 * n_p), 1.0, 0.0)
    for p, ref in enumerate(pieces):
        sel = jnp.where((row == p) & (col == h), 1.0, 0.0).astype(bf16)
        qa = qa + lax.dot_general(sel, ref[0, pl.ds(q0, FOX_BQ), :], (((1,), (1,)), ((), ())),
                                  preferred_element_type=f32)
    q_aug = jnp.concatenate([qT_ref[0, 0], qa.astype(bf16),
                             jnp.zeros((HEAD_DIM - AUG_ROWS, FOX_BQ), bf16)], axis=0)

    def scores(kb, q):
        start = pl.multiple_of(kb * FOX_BK, FOX_BK)
        return jnp.dot(kaug_ref[pl.ds(start, FOX_BK), :], q, preferred_element_type=f32)

    def softmax_step(s, m):
        m_new = jnp.maximum(m, jnp.max(s, axis=0, keepdims=True))
        return m_new, jnp.exp2(m - m_new), jnp.exp2(s - m_new).astype(bf16)

    def pv(kb, p):
        return jnp.dot(vT_ref[0, 0, kb], p, preferred_element_type=f32)

    n_slabs = FOX_BQ // MXU_N
    slabs = [slice(n * MXU_N, (n + 1) * MXU_N) for n in range(n_slabs)]
    q_slabs = [q_aug[:, c] for c in slabs]
    kk = lax.broadcasted_iota(jnp.int32, (FOX_BK, MXU_N), 0)
    qq = lax.broadcasted_iota(jnp.int32, (FOX_BK, MXU_N), 1)

    def run_units(units, lookahead):
        m = [m_ref[:, c] for c in slabs]
        acc = [acc_ref[:, c] for c in slabs]
        todo = units + lookahead
        s_val = {i: s_ref[:, slabs[i]] for i in range(FOX_LEAD)}
        soft = {}
        for x in range(len(units) + 1):
            if x + FOX_LEAD < len(todo):
                kb, n, _ = todo[x + FOX_LEAD]
                s_val[x + FOX_LEAD] = scores(kb, q_slabs[n])
            if x >= 1:
                kb, n, _ = units[x - 1]
                alpha, p = soft.pop(x - 1)
                acc[n] = alpha * acc[n] + pv(kb, p)
            if x < len(units):
                kb, n, diagonal = units[x]
                s = s_val.pop(x)
                if diagonal:
                    s = jnp.where(kk <= qq, s, -MASKED)
                m[n], alpha, p = softmax_step(s, m[n])
                soft[x] = (alpha, p)
        for i in range(len(lookahead)):
            s_ref[:, slabs[i]] = s_val[len(units) + i]
        for n, c in enumerate(slabs):
            m_ref[:, c], acc_ref[:, c] = m[n], acc[n]

    for i in range(FOX_LEAD):
        s_ref[:, slabs[i]] = scores(0, q_slabs[i])
    m_ref[...] = jnp.full_like(m_ref, -MASKED)
    acc_ref[...] = jnp.zeros_like(acc_ref)

    def body(j, carry):
        kb0 = j * FOX_TILES
        run_units([(kb0 + u, n, False) for u in range(FOX_TILES) for n in range(n_slabs)],
                  [(kb0 + FOX_TILES, n, False) for n in range(FOX_LEAD)])
        return carry

    lax.fori_loop(0, qi, body, 0)

    kb0 = qi * FOX_TILES
    run_units([(kb0 + d, n, n == d) for d in range(FOX_TILES) for n in range(d, n_slabs)], [])
    o = acc_ref[:HEAD_DIM, :] / acc_ref[HEAD_DIM:HEAD_DIM + 1, :]
    o_ref[0] = o.T.astype(bf16)


def fox_attention(qT, c_pieces, k, vT):
    nq = SEQ // FOX_BQ
    nk = SEQ // FOX_BK
    cspec = pl.BlockSpec((1, SEQ, HEAD_DIM), lambda b, h, i: (b, 0, 0))
    return pl.pallas_call(
        _fox_attn_kernel,
        out_shape=jax.ShapeDtypeStruct((BATCH, SEQ, D_MODEL), bf16),
        grid=(BATCH, FOX_HEADS, nq),
        in_specs=[pl.BlockSpec((1, 1, HEAD_DIM, FOX_BQ), lambda b, h, i: (b, h, 0, i)),
                  cspec, cspec, cspec,
                  pl.BlockSpec((1, 1, SEQ, HEAD_DIM), lambda b, h, i: (b, h, 0, 0)),
                  pl.BlockSpec((1, 1, nk, FOX_V_ROWS, FOX_BK), lambda b, h, i: (b, h, 0, 0, 0))],
        out_specs=pl.BlockSpec((1, FOX_BQ, HEAD_DIM), lambda b, h, i: (b, i, h)),
        scratch_shapes=[pltpu.VMEM((SEQ, AUG), bf16),
                        pltpu.VMEM((FOX_BK, FOX_LEAD * MXU_N), f32),
                        pltpu.VMEM((1, FOX_BQ), f32),
                        pltpu.VMEM((FOX_V_ROWS, FOX_BQ), f32)],
        compiler_params=_params(("arbitrary",) * 3, 48),
        name="fox_attention",
    )(qT, *c_pieces, k, vT)


OUT_BM = 512
OUT_SLAB = 512
LANES = 128
DIL_QUARTER = OUT_BM // DIL_RES


def _rmsnorm_rows(y, g):
    ms = jnp.mean(y * y, axis=-1, keepdims=True)
    return y * lax.rsqrt(ms + EPS) * g


def _outproj_kernel(o_ref, w_ref, x_ref, g_ref, y_ref, h_ref):
    for s in range(D_MODEL // OUT_SLAB):
        cols = slice(s * OUT_SLAB, (s + 1) * OUT_SLAB)
        y_ref[:, cols] = x_ref[:, cols] + jnp.dot(o_ref[...], w_ref[:, cols], preferred_element_type=f32)
    h_ref[...] = _rmsnorm_rows(y_ref[...], g_ref[...]).astype(bf16)


def _dil_outproj_kernel(o_ref, w_ref, x_ref, g_ref, y_ref, h_ref, nat_ref):
    o = o_ref[0].reshape(OUT_BM, D_MODEL)
    per = MXU_N // LANES
    for s in range(D_MODEL // MXU_N):
        mix = jnp.dot(o, w_ref[:, s * MXU_N:(s + 1) * MXU_N], preferred_element_type=f32)
        for u in range(per):
            c = s * per + u
            for rho in range(DIL_RES):
                nat_ref[c, pl.ds(rho, DIL_QUARTER, stride=DIL_RES), :] = (
                    mix[rho * DIL_QUARTER:(rho + 1) * DIL_QUARTER, u * LANES:(u + 1) * LANES])
            cols = slice(c * LANES, (c + 1) * LANES)
            y_ref[:, cols] = x_ref[:, cols] + nat_ref[c]
    h_ref[...] = _rmsnorm_rows(y_ref[...], g_ref[...]).astype(bf16)


def outproj_residual(o, w, x2d, g, residue_major):
    row = pl.BlockSpec((OUT_BM, D_MODEL), lambda i: (i, 0))
    if residue_major:
        quarters = DIL_TILE // OUT_BM
        o = o.reshape(TOKENS // DIL_TILE, DIL_RES, DIL_ROWS, D_MODEL)
        ospec = pl.BlockSpec((1, DIL_RES, DIL_QUARTER, D_MODEL), lambda i: (i // quarters, 0, i % quarters, 0))
        body, scratch = _dil_outproj_kernel, [pltpu.VMEM((D_MODEL // LANES, OUT_BM, LANES), f32)]
    else:
        ospec, body, scratch = row, _outproj_kernel, []
    return pl.pallas_call(
        body,
        out_shape=(jax.ShapeDtypeStruct((TOKENS, D_MODEL), f32), jax.ShapeDtypeStruct((TOKENS, D_MODEL), bf16)),
        grid=(TOKENS // OUT_BM,),
        in_specs=[ospec,
                  pl.BlockSpec((D_MODEL, D_MODEL), lambda i: (0, 0)),
                  row,
                  pl.BlockSpec((1, D_MODEL), lambda i: (0, 0))],
        out_specs=(row, row),
        scratch_shapes=scratch,
        compiler_params=_params(("arbitrary",), 52),
        name="dil_outproj_residual" if residue_major else "outproj_residual",
    )(o, w, x2d, g.reshape(1, D_MODEL))


MLP_BM = OUT_BM
MLP_BF = 1024


def _mlp_kernel(h_ref, wu_ref, wd_ref, x_ref, *refs, next_h):
    if next_h is None:
        (y_ref,) = refs
    else:
        g_ref, y_ref, hn_ref, slab_ref = refs
    f = pl.program_id(1)

    @pl.when(f == 0)
    def _():
        y_ref[...] = x_ref[...]

    a = jnp.maximum(jnp.dot(h_ref[...], wu_ref[...], preferred_element_type=f32), 0.0)
    y_ref[...] += jnp.dot((a * a).astype(bf16), wd_ref[...], preferred_element_type=f32)

    if next_h is not None:
        @pl.when(f == pl.num_programs(1) - 1)
        def _():
            hn = _rmsnorm_rows(y_ref[...], g_ref[...])
            for c in range(D_MODEL // LANES):
                slab_ref[c] = hn[:, c * LANES:(c + 1) * LANES]
            for rho in range(DIL_RES):
                for c in range(D_MODEL // LANES):
                    hn_ref[0, rho, :, c * LANES:(c + 1) * LANES] = (
                        slab_ref[c, pl.ds(rho, DIL_QUARTER, stride=DIL_RES), :].astype(bf16))


def mlp_residual(h2d, w_up, w_down, x2d, g_next=None):
    row = pl.BlockSpec((MLP_BM, D_MODEL), lambda i, f: (i, 0))
    in_specs = [row,
                pl.BlockSpec((D_MODEL, MLP_BF), lambda i, f: (0, f)),
                pl.BlockSpec((MLP_BF, D_MODEL), lambda i, f: (f, 0)),
                row]
    y_shape = jax.ShapeDtypeStruct((TOKENS, D_MODEL), f32)
    if g_next is None:
        args, out_shape, out_specs, scratch = (h2d, w_up, w_down, x2d), y_shape, row, []
    else:
        quarters = DIL_TILE // MLP_BM
        in_specs.append(pl.BlockSpec((1, D_MODEL), lambda i, f: (0, 0)))
        args = (h2d, w_up, w_down, x2d, g_next.reshape(1, D_MODEL))
        out_shape = (y_shape, jax.ShapeDtypeStruct((TOKENS // DIL_TILE, DIL_RES, DIL_ROWS, D_MODEL), bf16))
        out_specs = (row, pl.BlockSpec((1, DIL_RES, DIL_QUARTER, D_MODEL),
                                       lambda i, f: (i // quarters, 0, i % quarters, 0)))
        scratch = [pltpu.VMEM((D_MODEL // LANES, MLP_BM, LANES), f32)]
    return pl.pallas_call(
        functools.partial(_mlp_kernel, next_h=None if g_next is None else "residue_major"),
        out_shape=out_shape,
        grid=(TOKENS // MLP_BM, D_FF // MLP_BF),
        in_specs=in_specs,
        out_specs=out_specs,
        scratch_shapes=scratch,
        compiler_params=_params(("arbitrary", "arbitrary"), 56),
        name="mlp_residual",
    )(*args)


DIL_COLS = 2 * N_GROUPS * DIL_HEADS * HEAD_DIM + DIL_HEADS * DIL_V_DIM
DIL_QK_BLOCKS = 2 * N_GROUPS


def _dil_proj_kernel(h_ref, w_ref, g_ref, o_ref):
    j = pl.program_id(1)
    h = h_ref[...]

    def slabs(normalise):
        for s in range(PROJ_BN // MXU_N):
            y2 = jnp.dot(h, w_ref[:, s * MXU_N:(s + 1) * MXU_N], preferred_element_type=f32)
            if normalise:
                for u in range(MXU_N // HEAD_DIM):
                    cols = slice(s * MXU_N + u * HEAD_DIM, s * MXU_N + (u + 1) * HEAD_DIM)
                    o_ref[:, cols] = _head_rmsnorm(y2[:, u * HEAD_DIM:(u + 1) * HEAD_DIM], g_ref[0]).astype(bf16)
            else:
                o_ref[:, s * MXU_N:(s + 1) * MXU_N] = y2.astype(bf16)

    @pl.when(j < DIL_QK_BLOCKS)
    def _():
        slabs(True)

    @pl.when(j >= DIL_QK_BLOCKS)
    def _():
        slabs(False)


def dil_proj(h2d, w, gains):
    out = pl.pallas_call(
        _dil_proj_kernel,
        out_shape=jax.ShapeDtypeStruct((TOKENS, DIL_COLS), bf16),
        grid=(TOKENS // PROJ_BM, DIL_COLS // PROJ_BN),
        in_specs=[pl.BlockSpec((PROJ_BM, D_MODEL), lambda i, j: (i, 0)),
                  pl.BlockSpec((D_MODEL, PROJ_BN), lambda i, j: (0, j)),
                  pl.BlockSpec((1, 1, HEAD_DIM), lambda i, j: (j, 0, 0))],
        out_specs=pl.BlockSpec((PROJ_BM, PROJ_BN), lambda i, j: (i, j)),
        compiler_params=_params(("arbitrary", "arbitrary"), 48),
        name="dil_proj",
    )(h2d, w, gains)
    return out.reshape(BATCH, SEQ, DIL_COLS)


DIL_LEAD = 2


def _dil_distance_tables():
    def table(delta, prev_half, r):
        valid = (delta >= 0) & (delta <= DIL_SPAN)
        d = np.where(valid, delta * (r * LOG2E), MASKED).astype(np.float32)
        first = np.where(prev_half, np.float32(MASKED), d).astype(np.float32)
        return d, first

    i = np.arange(128)[:, None]
    j = np.arange(256)[None, :]
    t16 = table(128 + i - j, j < 128, 16)
    c, u = i // 32, i % 32
    cp, half, up = j // 64, (j % 64) // 32, j % 32
    t4 = table(128 + 4 * u + c - 128 * half - 4 * up - cp, half == 0, 4)
    i = np.arange(256)[:, None]
    j = np.arange(512)[None, :]
    rho, u = i // 16, i % 16
    rp, half, up = j // 32, (j % 32) // 16, j % 16
    t1 = table(256 + 16 * u + rho - 256 * half - 16 * up - rp, half == 0, 1)
    return t1, t4, t16


def _dil_attn_kernel(slopes_ref, q0_ref, q1_ref, q2_ref, k0_ref, k1_ref, k2_ref, v_ref,
                     d1_ref, d1f_ref, d4_ref, d4f_ref, d16_ref, d16f_ref,
                     o_ref, m_ref, l_ref, acc_ref):
    h = pl.program_id(1)
    t = pl.program_id(2)
    base = pl.multiple_of(t * DIL_TILE, DIL_TILE)
    pbase = pl.multiple_of(jnp.maximum(t - 1, 0) * DIL_TILE, DIL_TILE)
    first = t == 0

    def gather(ref, rows, size):
        return jnp.concatenate([ref[r:r + size, :] for r in rows], axis=0)

    def scores(u):
        q_ref, k_ref, bias, rows, size, n, _ = u
        q = jnp.concatenate([q_ref[0, r:r + size, :] for r in rows], axis=0)
        return lax.dot_general(q, window(k_ref, rows, size, n), (((1,), (1,)), ((), ())),
                               preferred_element_type=f32) + bias

    def softmax(u, s):
        _, _, _, rows, size, _, init = u
        m_new = jnp.max(s, axis=1, keepdims=True)
        m_old = None
        if not init:
            m_old = gather(m_ref, rows, size)
            m_new = jnp.maximum(m_old, m_new)
        p = jnp.exp2(s - m_new).astype(bf16)
        ones = jnp.ones((p.shape[1], LANES), bf16)
        return m_old, m_new, jnp.dot(p, ones, preferred_element_type=f32)[:, :1], p

    def accumulate(u, m_old, m_new, l_new, p):
        _, _, _, rows, size, n, init = u
        o_new = jnp.dot(p, window(v_ref, rows, size, n), preferred_element_type=f32)
        if not init:
            alpha = jnp.exp2(m_old - m_new)
            l_new = alpha * gather(l_ref, rows, size) + l_new
            o_new = alpha * gather(acc_ref, rows, size) + o_new
        for c, r in enumerate(rows):
            m_ref[r:r + size, :] = m_new[c * size:(c + 1) * size]
            l_ref[r:r + size, :] = l_new[c * size:(c + 1) * size]
            acc_ref[r:r + size, :] = o_new[c * size:(c + 1) * size]

    def window(ref, rows, size, n):
        if n > 0:
            return jnp.concatenate([ref[0, pl.ds(base + (r - size), 2 * size), :] for r in rows], axis=0)
        parts = []
        for r in rows:
            parts.append(ref[0, pl.ds(pbase + (r + DIL_ROWS - size), size), :])
            parts.append(ref[0, pl.ds(base + r, size), :])
        return jnp.concatenate(parts, axis=0)

    units = []
    slope = slopes_ref[2 * DIL_HEADS + h]
    bias = -slope * jnp.where(first, d16f_ref[...], d16_ref[...])
    for rho in range(DIL_RES):
        units.append((q2_ref, k2_ref, bias, [rho * DIL_ROWS], DIL_ROWS, 0, True))
    slope = slopes_ref[DIL_HEADS + h]
    bias_in = -slope * d4_ref[...]
    bias_edge = -slope * jnp.where(first, d4f_ref[...], d4_ref[...])
    for rho4 in range(4):
        for n in range(4):
            rows = [(rho4 + 4 * c) * DIL_ROWS + 32 * n for c in range(4)]
            units.append((q1_ref, k1_ref, bias_edge if n == 0 else bias_in, rows, 32, n, False))
    slope = slopes_ref[h]
    bias_in = -slope * d1_ref[...]
    bias_edge = -slope * jnp.where(first, d1f_ref[...], d1_ref[...])
    for n in range(DIL_ROWS // 16):
        rows = [rho * DIL_ROWS + 16 * n for rho in range(DIL_RES)]
        units.append((q0_ref, k0_ref, bias_edge if n == 0 else bias_in, rows, 16, n, False))

    s_val = {i: scores(units[i]) for i in range(DIL_LEAD)}
    soft = {}
    for x in range(len(units) + 1):
        if x + DIL_LEAD < len(units):
            s_val[x + DIL_LEAD] = scores(units[x + DIL_LEAD])
        if x >= 1:
            accumulate(units[x - 1], *soft.pop(x - 1))
        if x < len(units):
            soft[x] = softmax(units[x], s_val.pop(x))

    o_ref[0] = (acc_ref[...] / l_ref[...]).astype(bf16)


def dil_attention(p3, slopes):
    tables = [jnp.asarray(a) for pair in _dil_distance_tables() for a in pair]
    gh = DIL_HEADS
    kcol0 = N_GROUPS * gh
    vcol0 = 2 * N_GROUPS * gh * HEAD_DIM // DIL_V_DIM
    qspecs = [pl.BlockSpec((1, DIL_TILE, HEAD_DIM), lambda b, h, t, g=g: (b, t, g * gh + h))
              for g in range(N_GROUPS)]
    kspecs = [pl.BlockSpec((1, SEQ, HEAD_DIM), lambda b, h, t, g=g: (b, 0, kcol0 + g * gh + h))
              for g in range(N_GROUPS)]
    vspec = pl.BlockSpec((1, SEQ, DIL_V_DIM), lambda b, h, t: (b, 0, vcol0 + h))
    tspecs = [pl.BlockSpec(a.shape, lambda b, h, t: (0, 0)) for a in tables]
    return pl.pallas_call(
        _dil_attn_kernel,
        out_shape=jax.ShapeDtypeStruct((BATCH, SEQ, D_MODEL), bf16),
        grid=(BATCH, DIL_HEADS, SEQ // DIL_TILE),
        in_specs=[pl.BlockSpec(memory_space=pltpu.SMEM)] + qspecs + kspecs + [vspec] + tspecs,
        out_specs=pl.BlockSpec((1, DIL_TILE, DIL_V_DIM), lambda b, h, t: (b, t, h)),
        scratch_shapes=[pltpu.VMEM((DIL_TILE, 1), f32),
                        pltpu.VMEM((DIL_TILE, 1), f32),
                        pltpu.VMEM((DIL_TILE, DIL_V_DIM), f32)],
        compiler_params=_params(("arbitrary",) * 3, 48),
        name="dil_attention",
    )(slopes, p3, p3, p3, p3, p3, p3, p3, *tables)


def _fox_mixer(h2d, w_in, b_f, q_gain, k_gain):
    h3 = h2d.reshape(BATCH, SEQ, D_MODEL)
    wb, wf = to_bf16_split(w_in, 0, 3 * FOX_HEADS * HEAD_DIM)
    bfp = jnp.pad(b_f.astype(f32), (0, HEAD_DIM - FOX_HEADS)).reshape(1, HEAD_DIM)
    c_pieces = fox_gate(h3, wf, bfp)

    nb = D_MODEL // PROJ_BN
    qg = (q_gain * (QK_SCALE * LOG2E)).astype(f32).reshape(1, HEAD_DIM)
    kg = k_gain.astype(f32).reshape(1, HEAD_DIM)
    qT = fox_proj(h3, wb, qg, "q", 0)
    k = fox_proj(h3, wb, kg, "k", nb)
    vT = fox_proj(h3, wb, kg, "v", 2 * nb)
    return fox_attention(qT, c_pieces, k, vT).reshape(TOKENS, D_MODEL)


def _dil_mixer(h_rm, wb, q_gain, k_gain):
    ones = jnp.ones((DIL_COLS // PROJ_BN - DIL_QK_BLOCKS, HEAD_DIM), f32)
    gains = jnp.concatenate([q_gain.astype(f32) * (QK_SCALE * LOG2E), k_gain.astype(f32), ones], axis=0)
    p3 = dil_proj(h_rm, wb, gains.reshape(-1, 1, HEAD_DIM))
    n_heads = N_GROUPS * DIL_HEADS
    slopes = jnp.exp2(-ALIBI_MAX_EXP * jnp.arange(1, n_heads + 1, dtype=f32) / n_heads)
    return dil_attention(p3, slopes)


def kernel(x, fox_w_in, fox_b_f, fox_q_gain, fox_k_gain, fox_w_out, dil_w_in, dil_q_gain, dil_k_gain,
           dil_w_out, mix_norm_g, mlp_norm_g, mlp_w_up, mlp_w_down):
    x0 = x.reshape(TOKENS, D_MODEL)
    o = _fox_mixer(rmsnorm(x0, mix_norm_g[0]), fox_w_in, fox_b_f[0], fox_q_gain[0], fox_k_gain[0])
    x1, h = outproj_residual(o, to_bf16(fox_w_out, 0), x0, mlp_norm_g[0], residue_major=False)
    x2, h_rm = mlp_residual(h, to_bf16(mlp_w_up, 0), to_bf16(mlp_w_down, 0), x1, g_next=mix_norm_g[1])
    o = _dil_mixer(h_rm.reshape(TOKENS, D_MODEL), to_bf16(dil_w_in, 0), dil_q_gain[0], dil_k_gain[0])
    x3, h = outproj_residual(o, to_bf16(dil_w_out, 0), x2, mlp_norm_g[1], residue_major=True)
    x4 = mlp_residual(h, to_bf16(mlp_w_up, 1), to_bf16(mlp_w_down, 1), x3)
    return x4.reshape(BATCH, SEQ, D_MODEL)
```

```python
import functools

import numpy as np
import jax
import jax.numpy as jnp
from jax import lax
from jax.experimental import pallas as pl
from jax.experimental.pallas import tpu as pltpu

f32 = jnp.float32
bf16 = jnp.bfloat16

BATCH = 2
SEQ = 8192
D_MODEL = 2048
TOKENS = BATCH * SEQ
HEAD_DIM = 128
EPS = 1e-6
D_FF = 4 * D_MODEL
FOX_HEADS = D_MODEL // HEAD_DIM
DIL_PATTERNS = ((128, 1), (512, 4), (2048, 16))
N_GROUPS = len(DIL_PATTERNS)
DIL_SPAN = 128
DIL_HEADS = D_MODEL // (2 * HEAD_DIM)
DIL_V_DIM = D_MODEL // DIL_HEADS
ALIBI_MAX_EXP = 8.0

LOG2E = 1.4426950408889634
QK_SCALE = HEAD_DIM ** -0.5
MASKED = 1e30
MIB = 1024 * 1024

DIL_TILE = 2048
DIL_RES = 16
DIL_ROWS = DIL_TILE // DIL_RES


def _params(semantics, vmem_mib):
    return pltpu.CompilerParams(dimension_semantics=semantics, vmem_limit_bytes=vmem_mib * MIB)


def _rmsnorm_kernel(x_ref, g_ref, o_ref):
    x = x_ref[...]
    ms = jnp.mean(x * x, axis=-1, keepdims=True)
    o_ref[...] = (x * lax.rsqrt(ms + EPS) * g_ref[...]).astype(o_ref.dtype)


def rmsnorm(x2d, g):
    bm = 512
    return pl.pallas_call(
        _rmsnorm_kernel,
        out_shape=jax.ShapeDtypeStruct((TOKENS, D_MODEL), bf16),
        grid=(TOKENS // bm,),
        in_specs=[pl.BlockSpec((bm, D_MODEL), lambda i: (i, 0)),
                  pl.BlockSpec((1, D_MODEL), lambda i: (0, 0))],
        out_specs=pl.BlockSpec((bm, D_MODEL), lambda i: (i, 0)),
        compiler_params=_params(("arbitrary",), 32),
        name="rmsnorm",
    )(x2d, g.reshape(1, D_MODEL))


def _cast_kernel(w_ref, o_ref):
    o_ref[...] = w_ref[...].astype(o_ref.dtype)


def to_bf16(w3, layer):
    _, rows, cols = w3.shape
    br = 256 if cols > 4096 else 1024
    return pl.pallas_call(
        _cast_kernel,
        out_shape=jax.ShapeDtypeStruct((rows, cols), bf16),
        grid=(rows // br,),
        in_specs=[pl.BlockSpec((None, br, cols), lambda i: (layer, i, 0))],
        out_specs=pl.BlockSpec((br, cols), lambda i: (i, 0)),
        compiler_params=_params(("arbitrary",), 40),
        name="to_bf16",
    )(w3)


def _cast_split_kernel(w_ref, o_ref, t_ref):
    split = o_ref.shape[1]
    tail = w_ref.shape[1] - split
    o_ref[...] = w_ref[:, :split].astype(o_ref.dtype)
    t_ref[...] = jnp.zeros_like(t_ref)
    t_ref[:, :tail] = w_ref[:, split:].astype(t_ref.dtype)


def to_bf16_split(w3, layer, split):
    _, rows, cols = w3.shape
    br = 256
    return pl.pallas_call(
        _cast_split_kernel,
        out_shape=(jax.ShapeDtypeStruct((rows, split), bf16), jax.ShapeDtypeStruct((rows, HEAD_DIM), bf16)),
        grid=(rows // br,),
        in_specs=[pl.BlockSpec((None, br, cols), lambda i: (layer, i, 0))],
        out_specs=(pl.BlockSpec((br, split), lambda i: (i, 0)), pl.BlockSpec((br, HEAD_DIM), lambda i: (i, 0))),
        compiler_params=_params(("arbitrary",), 40),
        name="to_bf16_split",
    )(w3)


def _head_rmsnorm(y, gain):
    ms = jnp.mean(y * y, axis=-1, keepdims=True)
    return y * lax.rsqrt(ms + EPS) * gain


GATE_BM = 512


def _gate_kernel(h_ref, wf_ref, bf_ref, hi_ref, mid_ref, lo_ref, carry_ref):
    @pl.when(pl.program_id(1) == 0)
    def _():
        carry_ref[...] = jnp.zeros_like(carry_ref)

    f = jnp.dot(h_ref[0], wf_ref[...], preferred_element_type=f32) + bf_ref[...]
    lf = jnp.minimum(f, 0.0) - jnp.log1p(jnp.exp(-jnp.abs(f)))
    row = lax.broadcasted_iota(jnp.int32, (GATE_BM, GATE_BM), 0)
    col = lax.broadcasted_iota(jnp.int32, (GATE_BM, GATE_BM), 1)
    tri = (col <= row).astype(bf16)
    p0 = lf.astype(bf16)
    r0 = lf - p0.astype(f32)
    p1 = r0.astype(bf16)
    p2 = (r0 - p1.astype(f32)).astype(bf16)
    cs = (jnp.dot(tri, p0, preferred_element_type=f32)
          + jnp.dot(tri, p1, preferred_element_type=f32)
          + jnp.dot(tri, p2, preferred_element_type=f32))
    c = cs + carry_ref[...]
    carry_ref[...] = c[GATE_BM - 1:GATE_BM, :]
    c2 = c * LOG2E
    hi = c2.astype(bf16)
    r1 = c2 - hi.astype(f32)
    mid = r1.astype(bf16)
    lo = (r1 - mid.astype(f32)).astype(bf16)
    hi_ref[0] = hi
    mid_ref[0] = mid
    lo_ref[0] = lo


def fox_gate(h3, wf, bfp):
    piece = jax.ShapeDtypeStruct((BATCH, SEQ, HEAD_DIM), bf16)
    spec = pl.BlockSpec((1, GATE_BM, HEAD_DIM), lambda b, s: (b, s, 0))
    return pl.pallas_call(
        _gate_kernel,
        out_shape=(piece, piece, piece),
        grid=(BATCH, SEQ // GATE_BM),
        in_specs=[pl.BlockSpec((1, GATE_BM, D_MODEL), lambda b, s: (b, s, 0)),
                  pl.BlockSpec((D_MODEL, HEAD_DIM), lambda b, s: (0, 0)),
                  pl.BlockSpec((1, HEAD_DIM), lambda b, s: (0, 0))],
        out_specs=(spec, spec, spec),
        scratch_shapes=[pltpu.VMEM((1, HEAD_DIM), f32)],
        compiler_params=_params(("arbitrary", "arbitrary"), 32),
        name="fox_gate",
    )(h3, wf, bfp)


PROJ_BM = 1024
PROJ_BN = 1024
PROJ_HEADS = PROJ_BN // HEAD_DIM
MXU_N = 256
FOX_BK = 256
FOX_V_ROWS = HEAD_DIM + 16


def _fox_proj_kernel(h_ref, w_ref, g_ref, o_ref, *, mode):
    h = h_ref[0]
    if mode == "v":
        extra = lax.broadcasted_iota(jnp.int32, (FOX_V_ROWS - HEAD_DIM, FOX_BK), 0)
        ones_row = jnp.where(extra == 0, 1.0, 0.0).astype(bf16)
    for s in range(PROJ_BN // MXU_N):
        y2 = jnp.dot(h, w_ref[:, s * MXU_N:(s + 1) * MXU_N], preferred_element_type=f32)
        for u in range(MXU_N // HEAD_DIM):
            hh = s * (MXU_N // HEAD_DIM) + u
            y = y2[:, u * HEAD_DIM:(u + 1) * HEAD_DIM]
            if mode == "q":
                o_ref[0, hh] = _head_rmsnorm(y, g_ref[...]).T.astype(bf16)
            elif mode == "k":
                o_ref[0, hh] = _head_rmsnorm(y, g_ref[...]).astype(bf16)
            else:
                for c in range(PROJ_BM // FOX_BK):
                    o_ref[0, hh, c, :HEAD_DIM, :] = y[c * FOX_BK:(c + 1) * FOX_BK, :].T.astype(bf16)
                    o_ref[0, hh, c, HEAD_DIM:, :] = ones_row


def fox_proj(h3, w, gain, mode, col_block0):
    nsb = SEQ // PROJ_BM
    if mode == "q":
        shape = (BATCH, FOX_HEADS, HEAD_DIM, SEQ)
        ospec = pl.BlockSpec((1, PROJ_HEADS, HEAD_DIM, PROJ_BM), lambda b, s, j: (b, j, 0, s))
    elif mode == "k":
        shape = (BATCH, FOX_HEADS, SEQ, HEAD_DIM)
        ospec = pl.BlockSpec((1, PROJ_HEADS, PROJ_BM, HEAD_DIM), lambda b, s, j: (b, j, s, 0))
    else:
        shape = (BATCH, FOX_HEADS, SEQ // FOX_BK, FOX_V_ROWS, FOX_BK)
        ospec = pl.BlockSpec((1, PROJ_HEADS, PROJ_BM // FOX_BK, FOX_V_ROWS, FOX_BK),
                             lambda b, s, j: (b, j, s, 0, 0))
    return pl.pallas_call(
        functools.partial(_fox_proj_kernel, mode=mode),
        out_shape=jax.ShapeDtypeStruct(shape, bf16),
        grid=(BATCH, nsb, D_MODEL // PROJ_BN),
        in_specs=[pl.BlockSpec((1, PROJ_BM, D_MODEL), lambda b, s, j: (b, s, 0)),
                  pl.BlockSpec((D_MODEL, PROJ_BN), lambda b, s, j: (0, col_block0 + j)),
                  pl.BlockSpec((1, HEAD_DIM), lambda b, s, j: (0, 0))],
        out_specs=ospec,
        compiler_params=_params(("arbitrary",) * 3, 48),
        name="fox_proj_" + mode,
    )(h3, w, gain)


FOX_BQ = 2048
FOX_TILES = FOX_BQ // FOX_BK
FOX_LEAD = 4
AUG = 2 * HEAD_DIM
AUG_ROWS = 16
AUG_CHUNK = 1024


def _fox_attn_kernel(qT_ref, chi_ref, cmid_ref, clo_ref, k_ref, vT_ref, o_ref,
                     kaug_ref, s_ref, m_ref, acc_ref):
    h = pl.program_id(1)
    qi = pl.program_id(2)
    pieces = (chi_ref, cmid_ref, clo_ref)
    n_p = len(pieces)

    @pl.when(qi == 0)
    def _():
        kaug_ref[:, :HEAD_DIM] = k_ref[0, 0]
        row = lax.broadcasted_iota(jnp.int32, (HEAD_DIM, HEAD_DIM), 0)
        col = lax.broadcasted_iota(jnp.int32, (HEAD_DIM, HEAD_DIM), 1)
        lane = lax.broadcasted_iota(jnp.int32, (AUG_CHUNK, HEAD_DIM), 1)
        for c in range(SEQ // AUG_CHUNK):
            rows = slice(c * AUG_CHUNK, (c + 1) * AUG_CHUNK)
            aug = jnp.where(lane < n_p, 1.0, 0.0)
            for p, ref in enumerate(pieces):
                sel = jnp.where((row == h) & (col == n_p + p), -1.0, 0.0).astype(bf16)
                aug = aug + jnp.dot(ref[0, rows, :], sel, preferred_element_type=f32)
            kaug_ref[rows, HEAD_DIM:] = aug.astype(bf16)

    q0 = pl.multiple_of(qi * FOX_BQ, FOX_BQ)
    row = lax.broadcasted_iota(jnp.int32, (AUG_ROWS, HEAD_DIM), 0)
    col = lax.broadcasted_iota(jnp.int32, (AUG_ROWS, HEAD_DIM), 1)
    slot = lax.broadcasted_iota(jnp.int32, (AUG_ROWS, FOX_BQ), 0)
    qa = jnp.where((slot >= n_p) & (slot < 2 * n_p), 1.0, 0.0)
    for p, ref in enumerate(pieces):
        sel = jnp.where((row == p) & (col == h), 1.0, 0.0).astype(bf16)
        qa = qa + lax.dot_general(sel, ref[0, pl.ds(q0, FOX_BQ), :], (((1,), (1,)), ((), ())),
                                  preferred_element_type=f32)
    q_aug = jnp.concatenate([qT_ref[0, 0], qa.astype(bf16),
                             jnp.zeros((HEAD_DIM - AUG_ROWS, FOX_BQ), bf16)], axis=0)

    def scores(kb, q):
        start = pl.multiple_of(kb * FOX_BK, FOX_BK)
        return jnp.dot(kaug_ref[pl.ds(start, FOX_BK), :], q, preferred_element_type=f32)

    def softmax_step(s, m):
        m_new = jnp.maximum(m, jnp.max(s, axis=0, keepdims=True))
        return m_new, jnp.exp2(m - m_new), jnp.exp2(s - m_new).astype(bf16)

    def pv(kb, p):
        return jnp.dot(vT_ref[0, 0, kb], p, preferred_element_type=f32)

    n_slabs = FOX_BQ // MXU_N
    slabs = [slice(n * MXU_N, (n + 1) * MXU_N) for n in range(n_slabs)]
    q_slabs = [q_aug[:, c] for c in slabs]
    kk = lax.broadcasted_iota(jnp.int32, (FOX_BK, MXU_N), 0)
    qq = lax.broadcasted_iota(jnp.int32, (FOX_BK, MXU_N), 1)

    def run_units(units, lookahead):
        m = [m_ref[:, c] for c in slabs]
        acc = [acc_ref[:, c] for c in slabs]
        todo = units + lookahead
        s_val = {i: s_ref[:, slabs[i]] for i in range(FOX_LEAD)}
        soft = {}
        for x in range(len(units) + 1):
            if x + FOX_LEAD < len(todo):
                kb, n, _ = todo[x + FOX_LEAD]
                s_val[x + FOX_LEAD] = scores(kb, q_slabs[n])
            if x >= 1:
                kb, n, _ = units[x - 1]
                alpha, p = soft.pop(x - 1)
                acc[n] = alpha * acc[n] + pv(kb, p)
            if x < len(units):
                kb, n, diagonal = units[x]
                s = s_val.pop(x)
                if diagonal:
                    s = jnp.where(kk <= qq, s, -MASKED)
                m[n], alpha, p = softmax_step(s, m[n])
                soft[x] = (alpha, p)
        for i in range(len(lookahead)):
            s_ref[:, slabs[i]] = s_val[len(units) + i]
        for n, c in enumerate(slabs):
            m_ref[:, c], acc_ref[:, c] = m[n], acc[n]

    for i in range(FOX_LEAD):
        s_ref[:, slabs[i]] = scores(0, q_slabs[i])
    m_ref[...] = jnp.full_like(m_ref, -MASKED)
    acc_ref[...] = jnp.zeros_like(acc_ref)

    def body(j, carry):
        kb0 = j * FOX_TILES
        run_units([(kb0 + u, n, False) for u in range(FOX_TILES) for n in range(n_slabs)],
                  [(kb0 + FOX_TILES, n, False) for n in range(FOX_LEAD)])
        return carry

    lax.fori_loop(0, qi, body, 0)

    kb0 = qi * FOX_TILES
    run_units([(kb0 + d, n, n == d) for d in range(FOX_TILES) for n in range(d, n_slabs)], [])
    o = acc_ref[:HEAD_DIM, :] / acc_ref[HEAD_DIM:HEAD_DIM + 1, :]
    o_ref[0] = o.T.astype(bf16)


def fox_attention(qT, c_pieces, k, vT):
    nq = SEQ // FOX_BQ
    nk = SEQ // FOX_BK
    cspec = pl.BlockSpec((1, SEQ, HEAD_DIM), lambda b, h, i: (b, 0, 0))
    return pl.pallas_call(
        _fox_attn_kernel,
        out_shape=jax.ShapeDtypeStruct((BATCH, SEQ, D_MODEL), bf16),
        grid=(BATCH, FOX_HEADS, nq),
        in_specs=[pl.BlockSpec((1, 1, HEAD_DIM, FOX_BQ), lambda b, h, i: (b, h, 0, i)),
                  cspec, cspec, cspec,
                  pl.BlockSpec((1, 1, SEQ, HEAD_DIM), lambda b, h, i: (b, h, 0, 0)),
                  pl.BlockSpec((1, 1, nk, FOX_V_ROWS, FOX_BK), lambda b, h, i: (b, h, 0, 0, 0))],
        out_specs=pl.BlockSpec((1, FOX_BQ, HEAD_DIM), lambda b, h, i: (b, i, h)),
        scratch_shapes=[pltpu.VMEM((SEQ, AUG), bf16),
                        pltpu.VMEM((FOX_BK, FOX_LEAD * MXU_N), f32),
                        pltpu.VMEM((1, FOX_BQ), f32),
                        pltpu.VMEM((FOX_V_ROWS, FOX_BQ), f32)],
        compiler_params=_params(("arbitrary",) * 3, 48),
        name="fox_attention",
    )(qT, *c_pieces, k, vT)


OUT_BM = 512
OUT_SLAB = 512
LANES = 128
DIL_QUARTER = OUT_BM // DIL_RES


def _rmsnorm_rows(y, g):
    ms = jnp.mean(y * y, axis=-1, keepdims=True)
    return y * lax.rsqrt(ms + EPS) * g


def _outproj_kernel(o_ref, w_ref, x_ref, g_ref, y_ref, h_ref):
    for s in range(D_MODEL // OUT_SLAB):
        cols = slice(s * OUT_SLAB, (s + 1) * OUT_SLAB)
        y_ref[:, cols] = x_ref[:, cols] + jnp.dot(o_ref[...], w_ref[:, cols], preferred_element_type=f32)
    h_ref[...] = _rmsnorm_rows(y_ref[...], g_ref[...]).astype(bf16)


def _dil_outproj_kernel(o_ref, w_ref, x_ref, g_ref, y_ref, h_ref, nat_ref):
    o = o_ref[0].reshape(OUT_BM, D_MODEL)
    per = MXU_N // LANES
    for s in range(D_MODEL // MXU_N):
        mix = jnp.dot(o, w_ref[:, s * MXU_N:(s + 1) * MXU_N], preferred_element_type=f32)
        for u in range(per):
            c = s * per + u
            for rho in range(DIL_RES):
                nat_ref[c, pl.ds(rho, DIL_QUARTER, stride=DIL_RES), :] = (
                    mix[rho * DIL_QUARTER:(rho + 1) * DIL_QUARTER, u * LANES:(u + 1) * LANES])
            cols = slice(c * LANES, (c + 1) * LANES)
            y_ref[:, cols] = x_ref[:, cols] + nat_ref[c]
    h_ref[...] = _rmsnorm_rows(y_ref[...], g_ref[...]).astype(bf16)


def outproj_residual(o, w, x2d, g, residue_major):
    row = pl.BlockSpec((OUT_BM, D_MODEL), lambda i: (i, 0))
    if residue_major:
        quarters = DIL_TILE // OUT_BM
        o = o.reshape(TOKENS // DIL_TILE, DIL_RES, DIL_ROWS, D_MODEL)
        ospec = pl.BlockSpec((1, DIL_RES, DIL_QUARTER, D_MODEL), lambda i: (i // quarters, 0, i % quarters, 0))
        body, scratch = _dil_outproj_kernel, [pltpu.VMEM((D_MODEL // LANES, OUT_BM, LANES), f32)]
    else:
        ospec, body, scratch = row, _outproj_kernel, []
    return pl.pallas_call(
        body,
        out_shape=(jax.ShapeDtypeStruct((TOKENS, D_MODEL), f32), jax.ShapeDtypeStruct((TOKENS, D_MODEL), bf16)),
        grid=(TOKENS // OUT_BM,),
        in_specs=[ospec,
                  pl.BlockSpec((D_MODEL, D_MODEL), lambda i: (0, 0)),
                  row,
                  pl.BlockSpec((1, D_MODEL), lambda i: (0, 0))],
        out_specs=(row, row),
        scratch_shapes=scratch,
        compiler_params=_params(("arbitrary",), 52),
        name="dil_outproj_residual" if residue_major else "outproj_residual",
    )(o, w, x2d, g.reshape(1, D_MODEL))


MLP_BM = OUT_BM
MLP_BF = 1024


def _mlp_kernel(h_ref, wu_ref, wd_ref, x_ref, *refs, next_h):
    if next_h is None:
        (y_ref,) = refs
    else:
        g_ref, y_ref, hn_ref, slab_ref = refs
    f = pl.program_id(1)

    @pl.when(f == 0)
    def _():
        y_ref[...] = x_ref[...]

    a = jnp.maximum(jnp.dot(h_ref[...], wu_ref[...], preferred_element_type=f32), 0.0)
    y_ref[...] += jnp.dot((a * a).astype(bf16), wd_ref[...], preferred_element_type=f32)

    if next_h is not None:
        @pl.when(f == pl.num_programs(1) - 1)
        def _():
            hn = _rmsnorm_rows(y_ref[...], g_ref[...])
            for c in range(D_MODEL // LANES):
                slab_ref[c] = hn[:, c * LANES:(c + 1) * LANES]
            for rho in range(DIL_RES):
                for c in range(D_MODEL // LANES):
                    hn_ref[0, rho, :, c * LANES:(c + 1) * LANES] = (
                        slab_ref[c, pl.ds(rho, DIL_QUARTER, stride=DIL_RES), :].astype(bf16))


def mlp_residual(h2d, w_up, w_down, x2d, g_next=None):
    row = pl.BlockSpec((MLP_BM, D_MODEL), lambda i, f: (i, 0))
    in_specs = [row,
                pl.BlockSpec((D_MODEL, MLP_BF), lambda i, f: (0, f)),
                pl.BlockSpec((MLP_BF, D_MODEL), lambda i, f: (f, 0)),
                row]
    y_shape = jax.ShapeDtypeStruct((TOKENS, D_MODEL), f32)
    if g_next is None:
        args, out_shape, out_specs, scratch = (h2d, w_up, w_down, x2d), y_shape, row, []
    else:
        quarters = DIL_TILE // MLP_BM
        in_specs.append(pl.BlockSpec((1, D_MODEL), lambda i, f: (0, 0)))
        args = (h2d, w_up, w_down, x2d, g_next.reshape(1, D_MODEL))
        out_shape = (y_shape, jax.ShapeDtypeStruct((TOKENS // DIL_TILE, DIL_RES, DIL_ROWS, D_MODEL), bf16))
        out_specs = (row, pl.BlockSpec((1, DIL_RES, DIL_QUARTER, D_MODEL),
                                       lambda i, f: (i // quarters, 0, i % quarters, 0)))
        scratch = [pltpu.VMEM((D_MODEL // LANES, MLP_BM, LANES), f32)]
    return pl.pallas_call(
        functools.partial(_mlp_kernel, next_h=None if g_next is None else "residue_major"),
        out_shape=out_shape,
        grid=(TOKENS // MLP_BM, D_FF // MLP_BF),
        in_specs=in_specs,
        out_specs=out_specs,
        scratch_shapes=scratch,
        compiler_params=_params(("arbitrary", "arbitrary"), 56),
        name="mlp_residual",
    )(*args)


DIL_COLS = 2 * N_GROUPS * DIL_HEADS * HEAD_DIM + DIL_HEADS * DIL_V_DIM
DIL_QK_BLOCKS = 2 * N_GROUPS


def _dil_proj_kernel(h_ref, w_ref, g_ref, o_ref):
    j = pl.program_id(1)
    h = h_ref[...]

    def slabs(normalise):
        for s in range(PROJ_BN // MXU_N):
            y2 = jnp.dot(h, w_ref[:, s * MXU_N:(s + 1) * MXU_N], preferred_element_type=f32)
            if normalise:
                for u in range(MXU_N // HEAD_DIM):
                    cols = slice(s * MXU_N + u * HEAD_DIM, s * MXU_N + (u + 1) * HEAD_DIM)
                    o_ref[:, cols] = _head_rmsnorm(y2[:, u * HEAD_DIM:(u + 1) * HEAD_DIM], g_ref[0]).astype(bf16)
            else:
                o_ref[:, s * MXU_N:(s + 1) * MXU_N] = y2.astype(bf16)

    @pl.when(j < DIL_QK_BLOCKS)
    def _():
        slabs(True)

    @pl.when(j >= DIL_QK_BLOCKS)
    def _():
        slabs(False)


def dil_proj(h2d, w, gains):
    out = pl.pallas_call(
        _dil_proj_kernel,
        out_shape=jax.ShapeDtypeStruct((TOKENS, DIL_COLS), bf16),
        grid=(TOKENS // PROJ_BM, DIL_COLS // PROJ_BN),
        in_specs=[pl.BlockSpec((PROJ_BM, D_MODEL), lambda i, j: (i, 0)),
                  pl.BlockSpec((D_MODEL, PROJ_BN), lambda i, j: (0, j)),
                  pl.BlockSpec((1, 1, HEAD_DIM), lambda i, j: (j, 0, 0))],
        out_specs=pl.BlockSpec((PROJ_BM, PROJ_BN), lambda i, j: (i, j)),
        compiler_params=_params(("arbitrary", "arbitrary"), 48),
        name="dil_proj",
    )(h2d, w, gains)
    return out.reshape(BATCH, SEQ, DIL_COLS)


DIL_LEAD = 2


def _dil_distance_tables():
    def table(delta, prev_half, r):
        valid = (delta >= 0) & (delta <= DIL_SPAN)
        d = np.where(valid, delta * (r * LOG2E), MASKED).astype(np.float32)
        first = np.where(prev_half, np.float32(MASKED), d).astype(np.float32)
        return d, first

    i = np.arange(128)[:, None]
    j = np.arange(256)[None, :]
    t16 = table(128 + i - j, j < 128, 16)
    c, u = i // 32, i % 32
    cp, half, up = j // 64, (j % 64) // 32, j % 32
    t4 = table(128 + 4 * u + c - 128 * half - 4 * up - cp, half == 0, 4)
    i = np.arange(256)[:, None]
    j = np.arange(512)[None, :]
    rho, u = i // 16, i % 16
    rp, half, up = j // 32, (j % 32) // 16, j % 16
    t1 = table(256 + 16 * u + rho - 256 * half - 16 * up - rp, half == 0, 1)
    return t1, t4, t16


def _dil_attn_kernel(slopes_ref, q0_ref, q1_ref, q2_ref, k0_ref, k1_ref, k2_ref, v_ref,
                     d1_ref, d1f_ref, d4_ref, d4f_ref, d16_ref, d16f_ref,
                     o_ref, m_ref, l_ref, acc_ref):
    h = pl.program_id(1)
    t = pl.program_id(2)
    base = pl.multiple_of(t * DIL_TILE, DIL_TILE)
    pbase = pl.multiple_of(jnp.maximum(t - 1, 0) * DIL_TILE, DIL_TILE)
    first = t == 0

    def gather(ref, rows, size):
        return jnp.concatenate([ref[r:r + size, :] for r in rows], axis=0)

    def scores(u):
        q_ref, k_ref, bias, rows, size, n, _ = u
        q = jnp.concatenate([q_ref[0, r:r + size, :] for r in rows], axis=0)
        return lax.dot_general(q, window(k_ref, rows, size, n), (((1,), (1,)), ((), ())),
                               preferred_element_type=f32) + bias

    def softmax(u, s):
        _, _, _, rows, size, _, init = u
        m_new = jnp.max(s, axis=1, keepdims=True)
        m_old = None
        if not init:
            m_old = gather(m_ref, rows, size)
            m_new = jnp.maximum(m_old, m_new)
        p = jnp.exp2(s - m_new).astype(bf16)
        ones = jnp.ones((p.shape[1], LANES), bf16)
        return m_old, m_new, jnp.dot(p, ones, preferred_element_type=f32)[:, :1], p

    def accumulate(u, m_old, m_new, l_new, p):
        _, _, _, rows, size, n, init = u
        o_new = jnp.dot(p, window(v_ref, rows, size, n), preferred_element_type=f32)
        if not init:
            alpha = jnp.exp2(m_old - m_new)
            l_new = alpha * gather(l_ref, rows, size) + l_new
            o_new = alpha * gather(acc_ref, rows, size) + o_new
        for c, r in enumerate(rows):
            m_ref[r:r + size, :] = m_new[c * size:(c + 1) * size]
            l_ref[r:r + size, :] = l_new[c * size:(c + 1) * size]
            acc_ref[r:r + size, :] = o_new[c * size:(c + 1) * size]

    def window(ref, rows, size, n):
        if n > 0:
            return jnp.concatenate([ref[0, pl.ds(base + (r - size), 2 * size), :] for r in rows], axis=0)
        parts = []
        for r in rows:
            parts.append(ref[0, pl.ds(pbase + (r + DIL_ROWS - size), size), :])
            parts.append(ref[0, pl.ds(base + r, size), :])
        return jnp.concatenate(parts, axis=0)

    units = []
    slope = slopes_ref[2 * DIL_HEADS + h]
    bias = -slope * jnp.where(first, d16f_ref[...], d16_ref[...])
    for rho in range(DIL_RES):
        units.append((q2_ref, k2_ref, bias, [rho * DIL_ROWS], DIL_ROWS, 0, True))
    slope = slopes_ref[DIL_HEADS + h]
    bias_in = -slope * d4_ref[...]
    bias_edge = -slope * jnp.where(first, d4f_ref[...], d4_ref[...])
    for rho4 in range(4):
        for n in range(4):
            rows = [(rho4 + 4 * c) * DIL_ROWS + 32 * n for c in range(4)]
            units.append((q1_ref, k1_ref, bias_edge if n == 0 else bias_in, rows, 32, n, False))
    slope = slopes_ref[h]
    bias_in = -slope * d1_ref[...]
    bias_edge = -slope * jnp.where(first, d1f_ref[...], d1_ref[...])
    for n in range(DIL_ROWS // 16):
        rows = [rho * DIL_ROWS + 16 * n for rho in range(DIL_RES)]
        units.append((q0_ref, k0_ref, bias_edge if n == 0 else bias_in, rows, 16, n, False))

    s_val = {i: scores(units[i]) for i in range(DIL_LEAD)}
    soft = {}
    for x in range(len(units) + 1):
        if x + DIL_LEAD < len(units):
            s_val[x + DIL_LEAD] = scores(units[x + DIL_LEAD])
        if x >= 1:
            accumulate(units[x - 1], *soft.pop(x - 1))
        if x < len(units):
            soft[x] = softmax(units[x], s_val.pop(x))

    o_ref[0] = (acc_ref[...] / l_ref[...]).astype(bf16)


def dil_attention(p3, slopes):
    tables = [jnp.asarray(a) for pair in _dil_distance_tables() for a in pair]
    gh = DIL_HEADS
    kcol0 = N_GROUPS * gh
    vcol0 = 2 * N_GROUPS * gh * HEAD_DIM // DIL_V_DIM
    qspecs = [pl.BlockSpec((1, DIL_TILE, HEAD_DIM), lambda b, h, t, g=g: (b, t, g * gh + h))
              for g in range(N_GROUPS)]
    kspecs = [pl.BlockSpec((1, SEQ, HEAD_DIM), lambda b, h, t, g=g: (b, 0, kcol0 + g * gh + h))
              for g in range(N_GROUPS)]
    vspec = pl.BlockSpec((1, SEQ, DIL_V_DIM), lambda b, h, t: (b, 0, vcol0 + h))
    tspecs = [pl.BlockSpec(a.shape, lambda b, h, t: (0, 0)) for a in tables]
    return pl.pallas_call(
        _dil_attn_kernel,
        out_shape=jax.ShapeDtypeStruct((BATCH, SEQ, D_MODEL), bf16),
        grid=(BATCH, DIL_HEADS, SEQ // DIL_TILE),
        in_specs=[pl.BlockSpec(memory_space=pltpu.SMEM)] + qspecs + kspecs + [vspec] + tspecs,
        out_specs=pl.BlockSpec((1, DIL_TILE, DIL_V_DIM), lambda b, h, t: (b, t, h)),
        scratch_shapes=[pltpu.VMEM((DIL_TILE, 1), f32),
                        pltpu.VMEM((DIL_TILE, 1), f32),
                        pltpu.VMEM((DIL_TILE, DIL_V_DIM), f32)],
        compiler_params=_params(("arbitrary",) * 3, 48),
        name="dil_attention",
    )(slopes, p3, p3, p3, p3, p3, p3, p3, *tables)


def _fox_mixer(h2d, w_in, b_f, q_gain, k_gain):
    h3 = h2d.reshape(BATCH, SEQ, D_MODEL)
    wb, wf = to_bf16_split(w_in, 0, 3 * FOX_HEADS * HEAD_DIM)
    bfp = jnp.pad(b_f.astype(f32), (0, HEAD_DIM - FOX_HEADS)).reshape(1, HEAD_DIM)
    c_pieces = fox_gate(h3, wf, bfp)

    nb = D_MODEL // PROJ_BN
    qg = (q_gain * (QK_SCALE * LOG2E)).astype(f32).reshape(1, HEAD_DIM)
    kg = k_gain.astype(f32).reshape(1, HEAD_DIM)
    qT = fox_proj(h3, wb, qg, "q", 0)
    k = fox_proj(h3, wb, kg, "k", nb)
    vT = fox_proj(h3, wb, kg, "v", 2 * nb)
    return fox_attention(qT, c_pieces, k, vT).reshape(TOKENS, D_MODEL)


def _dil_mixer(h_rm, wb, q_gain, k_gain):
    ones = jnp.ones((DIL_COLS // PROJ_BN - DIL_QK_BLOCKS, HEAD_DIM), f32)
    gains = jnp.concatenate([q_gain.astype(f32) * (QK_SCALE * LOG2E), k_gain.astype(f32), ones], axis=0)
    p3 = dil_proj(h_rm, wb, gains.reshape(-1, 1, HEAD_DIM))
    n_heads = N_GROUPS * DIL_HEADS
    slopes = jnp.exp2(-ALIBI_MAX_EXP * jnp.arange(1, n_heads + 1, dtype=f32) / n_heads)
    return dil_attention(p3, slopes)


def kernel(x, fox_w_in, fox_b_f, fox_q_gain, fox_k_gain, fox_w_out, dil_w_in, dil_q_gain, dil_k_gain,
           dil_w_out, mix_norm_g, mlp_norm_g, mlp_w_up, mlp_w_down):
    x0 = x.reshape(TOKENS, D_MODEL)
    o = _fox_mixer(rmsnorm(x0, mix_norm_g[0]), fox_w_in, fox_b_f[0], fox_q_gain[0], fox_k_gain[0])
    x1, h = outproj_residual(o, to_bf16(fox_w_out, 0), x0, mlp_norm_g[0], residue_major=False)
    x2, h_rm = mlp_residual(h, to_bf16(mlp_w_up, 0), to_bf16(mlp_w_down, 0), x1, g_next=mix_norm_g[1])
    o = _dil_mixer(h_rm.reshape(TOKENS, D_MODEL), to_bf16(dil_w_in, 0), dil_q_gain[0], dil_k_gain[0])
    x3, h = outproj_residual(o, to_bf16(dil_w_out, 0), x2, mlp_norm_g[1], residue_major=True)
    x4 = mlp_residual(h, to_bf16(mlp_w_up, 1), to_bf16(mlp_w_down, 1), x3)
    return x4.reshape(BATCH, SEQ, D_MODEL)
```

```python
import functools

import numpy as np
import jax
import jax.numpy as jnp
from jax import lax
from jax.experimental import pallas as pl
from jax.experimental.pallas import tpu as pltpu

f32 = jnp.float32
bf16 = jnp.bfloat16

BATCH = 2
SEQ = 8192
D_MODEL = 2048
TOKENS = BATCH * SEQ
HEAD_DIM = 128
EPS = 1e-6
D_FF = 4 * D_MODEL
FOX_HEADS = D_MODEL // HEAD_DIM
DIL_PATTERNS = ((128, 1), (512, 4), (2048, 16))
N_GROUPS = len(DIL_PATTERNS)
DIL_SPAN = 128
DIL_HEADS = D_MODEL // (2 * HEAD_DIM)
DIL_V_DIM = D_MODEL // DIL_HEADS
ALIBI_MAX_EXP = 8.0

LOG2E = 1.4426950408889634
QK_SCALE = HEAD_DIM ** -0.5
MASKED = 1e30
MIB = 1024 * 1024

DIL_TILE = 2048
DIL_RES = 16
DIL_ROWS = DIL_TILE // DIL_RES


def _params(semantics, vmem_mib):
    return pltpu.CompilerParams(dimension_semantics=semantics, vmem_limit_bytes=vmem_mib * MIB)


def _cast_kernel(w_ref, o_ref):
    o_ref[...] = w_ref[...].astype(o_ref.dtype)


def to_bf16(w3, layer):
    _, rows, cols = w3.shape
    br = 256 if cols > 4096 else 1024
    return pl.pallas_call(
        _cast_kernel,
        out_shape=jax.ShapeDtypeStruct((rows, cols), bf16),
        grid=(rows // br,),
        in_specs=[pl.BlockSpec((None, br, cols), lambda i: (layer, i, 0))],
        out_specs=pl.BlockSpec((br, cols), lambda i: (i, 0)),
        compiler_params=_params(("arbitrary",), 40),
        name="to_bf16",
    )(w3)


def _cast_transpose_kernel(wt_ref, o_ref):
    o_ref[...] = wt_ref[...].T.astype(o_ref.dtype)


def _cast_tail_kernel(wt_ref, t_ref):
    t_ref[...] = jnp.zeros_like(t_ref)
    t_ref[:wt_ref.shape[0], :] = wt_ref[...].astype(t_ref.dtype)


def to_bf16_split(w3, layer, split):
    _, rows, cols = w3.shape
    tail = cols - split
    wt = jnp.swapaxes(w3, 1, 2)
    bc = 256
    main = pl.pallas_call(
        _cast_transpose_kernel,
        out_shape=jax.ShapeDtypeStruct((rows, split), bf16),
        grid=(split // bc,),
        in_specs=[pl.BlockSpec((None, bc, rows), lambda i: (layer, i, 0))],
        out_specs=pl.BlockSpec((rows, bc), lambda i: (0, i)),
        compiler_params=_params(("arbitrary",), 40),
        name="to_bf16_transposed",
    )(wt)
    rest = pl.pallas_call(
        _cast_tail_kernel,
        out_shape=jax.ShapeDtypeStruct((HEAD_DIM, rows), bf16),
        grid=(1,),
        in_specs=[pl.BlockSpec((None, tail, rows), lambda i: (layer, split // tail, 0))],
        out_specs=pl.BlockSpec((HEAD_DIM, rows), lambda i: (0, 0)),
        compiler_params=_params(("arbitrary",), 40),
        name="to_bf16_tail",
    )(wt)
    return main, rest


def _head_rmsnorm(y, gain):
    ms = jnp.mean(y * y, axis=-1, keepdims=True)
    return y * lax.rsqrt(ms + EPS) * gain


GATE_BM = 512


def _gate_kernel(x_ref, g_ref, wf_ref, bf_ref, h_ref, hi_ref, mid_ref, lo_ref, carry_ref):
    @pl.when(pl.program_id(1) == 0)
    def _():
        carry_ref[...] = jnp.zeros_like(carry_ref)

    x = x_ref[0]
    h = (x * lax.rsqrt(jnp.mean(x * x, axis=-1, keepdims=True) + EPS) * g_ref[...]).astype(bf16)
    h_ref[0] = h
    f = lax.dot_general(h, wf_ref[...], (((1,), (1,)), ((), ())), preferred_element_type=f32) + bf_ref[...]
    lf = jnp.minimum(f, 0.0) - jnp.log1p(jnp.exp(-jnp.abs(f)))
    row = lax.broadcasted_iota(jnp.int32, (GATE_BM, GATE_BM), 0)
    col = lax.broadcasted_iota(jnp.int32, (GATE_BM, GATE_BM), 1)
    tri = (col <= row).astype(bf16)
    p0 = lf.astype(bf16)
    r0 = lf - p0.astype(f32)
    p1 = r0.astype(bf16)
    p2 = (r0 - p1.astype(f32)).astype(bf16)
    cs = (jnp.dot(tri, p0, preferred_element_type=f32)
          + jnp.dot(tri, p1, preferred_element_type=f32)
          + jnp.dot(tri, p2, preferred_element_type=f32))
    c = cs + carry_ref[...]
    carry_ref[...] = c[GATE_BM - 1:GATE_BM, :]
    c2 = c * LOG2E
    hi = c2.astype(bf16)
    r1 = c2 - hi.astype(f32)
    mid = r1.astype(bf16)
    lo = (r1 - mid.astype(f32)).astype(bf16)
    hi_ref[0] = hi
    mid_ref[0] = mid
    lo_ref[0] = lo


def fox_norm_gate(x3, g, wf, bfp):
    piece = jax.ShapeDtypeStruct((BATCH, SEQ, HEAD_DIM), bf16)
    spec = pl.BlockSpec((1, GATE_BM, HEAD_DIM), lambda b, s: (b, s, 0))
    rows = pl.BlockSpec((1, GATE_BM, D_MODEL), lambda b, s: (b, s, 0))
    return pl.pallas_call(
        _gate_kernel,
        out_shape=(jax.ShapeDtypeStruct((BATCH, SEQ, D_MODEL), bf16), piece, piece, piece),
        grid=(BATCH, SEQ // GATE_BM),
        in_specs=[rows,
                  pl.BlockSpec((1, D_MODEL), lambda b, s: (0, 0)),
                  pl.BlockSpec((HEAD_DIM, D_MODEL), lambda b, s: (0, 0)),
                  pl.BlockSpec((1, HEAD_DIM), lambda b, s: (0, 0))],
        out_specs=(rows, spec, spec, spec),
        scratch_shapes=[pltpu.VMEM((1, HEAD_DIM), f32)],
        compiler_params=_params(("arbitrary", "arbitrary"), 32),
        name="fox_norm_gate",
    )(x3, g.reshape(1, D_MODEL), wf, bfp)


PROJ_BM = 1024
PROJ_BN = 1024
PROJ_HEADS = PROJ_BN // HEAD_DIM
MXU_N = 256
FOX_BK = 256
FOX_V_ROWS = HEAD_DIM + 16


def _fox_proj_kernel(h_ref, w_ref, g_ref, o_ref, *, mode):
    h = h_ref[0]
    if mode == "v":
        extra = lax.broadcasted_iota(jnp.int32, (FOX_V_ROWS - HEAD_DIM, FOX_BK), 0)
        ones_row = jnp.where(extra == 0, 1.0, 0.0).astype(bf16)
    for s in range(PROJ_BN // MXU_N):
        y2 = jnp.dot(h, w_ref[:, s * MXU_N:(s + 1) * MXU_N], preferred_element_type=f32)
        for u in range(MXU_N // HEAD_DIM):
            hh = s * (MXU_N // HEAD_DIM) + u
            y = y2[:, u * HEAD_DIM:(u + 1) * HEAD_DIM]
            if mode == "q":
                o_ref[0, hh] = _head_rmsnorm(y, g_ref[...]).T.astype(bf16)
            elif mode == "k":
                o_ref[0, hh] = _head_rmsnorm(y, g_ref[...]).astype(bf16)
            else:
                for c in range(PROJ_BM // FOX_BK):
                    o_ref[0, hh, c, :HEAD_DIM, :] = y[c * FOX_BK:(c + 1) * FOX_BK, :].T.astype(bf16)
                    o_ref[0, hh, c, HEAD_DIM:, :] = ones_row


def fox_proj(h3, w, gain, mode, col_block0):
    nsb = SEQ // PROJ_BM
    if mode == "q":
        shape = (BATCH, FOX_HEADS, HEAD_DIM, SEQ)
        ospec = pl.BlockSpec((1, PROJ_HEADS, HEAD_DIM, PROJ_BM), lambda b, s, j: (b, j, 0, s))
    elif mode == "k":
        shape = (BATCH, FOX_HEADS, SEQ, HEAD_DIM)
        ospec = pl.BlockSpec((1, PROJ_HEADS, PROJ_BM, HEAD_DIM), lambda b, s, j: (b, j, s, 0))
    else:
        shape = (BATCH, FOX_HEADS, SEQ // FOX_BK, FOX_V_ROWS, FOX_BK)
        ospec = pl.BlockSpec((1, PROJ_HEADS, PROJ_BM // FOX_BK, FOX_V_ROWS, FOX_BK),
                             lambda b, s, j: (b, j, s, 0, 0))
    return pl.pallas_call(
        functools.partial(_fox_proj_kernel, mode=mode),
        out_shape=jax.ShapeDtypeStruct(shape, bf16),
        grid=(BATCH, nsb, D_MODEL // PROJ_BN),
        in_specs=[pl.BlockSpec((1, PROJ_BM, D_MODEL), lambda b, s, j: (b, s, 0)),
                  pl.BlockSpec((D_MODEL, PROJ_BN), lambda b, s, j: (0, col_block0 + j)),
                  pl.BlockSpec((1, HEAD_DIM), lambda b, s, j: (0, 0))],
        out_specs=ospec,
        compiler_params=_params(("arbitrary",) * 3, 48),
        name="fox_proj_" + mode,
    )(h3, w, gain)


FOX_BQ = 2048
FOX_TILES = FOX_BQ // FOX_BK
FOX_LEAD = 4
AUG = 2 * HEAD_DIM
AUG_ROWS = 16
AUG_CHUNK = 1024


def _fox_attn_kernel(qT_ref, chi_ref, cmid_ref, clo_ref, k_ref, vT_ref, o_ref,
                     kaug_ref, s_ref, m_ref, acc_ref):
    h = pl.program_id(1)
    qi = pl.program_id(2)
    pieces = (chi_ref, cmid_ref, clo_ref)
    n_p = len(pieces)

    @pl.when(qi == 0)
    def _():
        kaug_ref[:, :HEAD_DIM] = k_ref[0, 0]
        row = lax.broadcasted_iota(jnp.int32, (HEAD_DIM, HEAD_DIM), 0)
        col = lax.broadcasted_iota(jnp.int32, (HEAD_DIM, HEAD_DIM), 1)
        lane = lax.broadcasted_iota(jnp.int32, (AUG_CHUNK, HEAD_DIM), 1)
        for c in range(SEQ // AUG_CHUNK):
            rows = slice(c * AUG_CHUNK, (c + 1) * AUG_CHUNK)
            aug = jnp.where(lane < n_p, 1.0, 0.0)
            for p, ref in enumerate(pieces):
                sel = jnp.where((row == h) & (col == n_p + p), -1.0, 0.0).astype(bf16)
                aug = aug + jnp.dot(ref[0, rows, :], sel, preferred_element_type=f32)
            kaug_ref[rows, HEAD_DIM:] = aug.astype(bf16)

    q0 = pl.multiple_of(qi * FOX_BQ, FOX_BQ)
    row = lax.broadcasted_iota(jnp.int32, (AUG_ROWS, HEAD_DIM), 0)
    col = lax.broadcasted_iota(jnp.int32, (AUG_ROWS, HEAD_DIM), 1)
    slot = lax.broadcasted_iota(jnp.int32, (AUG_ROWS, FOX_BQ), 0)
    qa = jnp.where((slot >= n_p) & (slot < 2 * n_p), 1.0, 0.0)
    for p, ref in enumerate(pieces):
        sel = jnp.where((row == p) & (col == h), 1.0, 0.0).astype(bf16)
        qa = qa + lax.dot_general(sel, ref[0, pl.ds(q0, FOX_BQ), :], (((1,), (1,)), ((), ())),
                                  preferred_element_type=f32)
    q_aug = jnp.concatenate([qT_ref[0, 0], qa.astype(bf16),
                             jnp.zeros((HEAD_DIM - AUG_ROWS, FOX_BQ), bf16)], axis=0)

    def scores(kb, q):
        start = pl.multiple_of(kb * FOX_BK, FOX_BK)
        return jnp.dot(kaug_ref[pl.ds(start, FOX_BK), :], q, preferred_element_type=f32)

    def softmax_step(s, m):
        m_new = jnp.maximum(m, jnp.max(s, axis=0, keepdims=True))
        return m_new, jnp.exp2(m - m_new), jnp.exp2(s - m_new).astype(bf16)

    def pv(kb, p):
        return jnp.dot(vT_ref[0, 0, kb], p, preferred_element_type=f32)

    n_slabs = FOX_BQ // MXU_N
    slabs = [slice(n * MXU_N, (n + 1) * MXU_N) for n in range(n_slabs)]
    q_slabs = [q_aug[:, c] for c in slabs]
    kk = lax.broadcasted_iota(jnp.int32, (FOX_BK, MXU_N), 0)
    qq = lax.broadcasted_iota(jnp.int32, (FOX_BK, MXU_N), 1)

    def run_units(units, lookahead):
        m = [m_ref[:, c] for c in slabs]
        acc = [acc_ref[:, c] for c in slabs]
        todo = units + lookahead
        s_val = {i: s_ref[:, slabs[i]] for i in range(FOX_LEAD)}
        soft = {}
        for x in range(len(units) + 1):
            if x + FOX_LEAD < len(todo):
                kb, n, _ = todo[x + FOX_LEAD]
                s_val[x + FOX_LEAD] = scores(kb, q_slabs[n])
            if x >= 1:
                kb, n, _ = units[x - 1]
                alpha, p = soft.pop(x - 1)
                acc[n] = alpha * acc[n] + pv(kb, p)
            if x < len(units):
                kb, n, diagonal = units[x]
                s = s_val.pop(x)
                if diagonal:
                    s = jnp.where(kk <= qq, s, -MASKED)
                m[n], alpha, p = softmax_step(s, m[n])
                soft[x] = (alpha, p)
        for i in range(len(lookahead)):
            s_ref[:, slabs[i]] = s_val[len(units) + i]
        for n, c in enumerate(slabs):
            m_ref[:, c], acc_ref[:, c] = m[n], acc[n]

    for i in range(FOX_LEAD):
        s_ref[:, slabs[i]] = scores(0, q_slabs[i])
    m_ref[...] = jnp.full_like(m_ref, -MASKED)
    acc_ref[...] = jnp.zeros_like(acc_ref)

    def body(j, carry):
        kb0 = j * FOX_TILES
        run_units([(kb0 + u, n, False) for u in range(FOX_TILES) for n in range(n_slabs)],
                  [(kb0 + FOX_TILES, n, False) for n in range(FOX_LEAD)])
        return carry

    lax.fori_loop(0, qi, body, 0)

    kb0 = qi * FOX_TILES
    run_units([(kb0 + d, n, n == d) for d in range(FOX_TILES) for n in range(d, n_slabs)], [])
    o = acc_ref[:HEAD_DIM, :] / acc_ref[HEAD_DIM:HEAD_DIM + 1, :]
    o_ref[0] = o.T.astype(bf16)


def fox_attention(qT, c_pieces, k, vT):
    nq = SEQ // FOX_BQ
    nk = SEQ // FOX_BK
    cspec = pl.BlockSpec((1, SEQ, HEAD_DIM), lambda b, h, i: (b, 0, 0))
    return pl.pallas_call(
        _fox_attn_kernel,
        out_shape=jax.ShapeDtypeStruct((BATCH, SEQ, D_MODEL), bf16),
        grid=(BATCH, FOX_HEADS, nq),
        in_specs=[pl.BlockSpec((1, 1, HEAD_DIM, FOX_BQ), lambda b, h, i: (b, h, 0, i)),
                  cspec, cspec, cspec,
                  pl.BlockSpec((1, 1, SEQ, HEAD_DIM), lambda b, h, i: (b, h, 0, 0)),
                  pl.BlockSpec((1, 1, nk, FOX_V_ROWS, FOX_BK), lambda b, h, i: (b, h, 0, 0, 0))],
        out_specs=pl.BlockSpec((1, FOX_BQ, HEAD_DIM), lambda b, h, i: (b, i, h)),
        scratch_shapes=[pltpu.VMEM((SEQ, AUG), bf16),
                        pltpu.VMEM((FOX_BK, FOX_LEAD * MXU_N), f32),
                        pltpu.VMEM((1, FOX_BQ), f32),
                        pltpu.VMEM((FOX_V_ROWS, FOX_BQ), f32)],
        compiler_params=_params(("arbitrary",) * 3, 48),
        name="fox_attention",
    )(qT, *c_pieces, k, vT)


OUT_BM = 512
OUT_SLAB = 512
LANES = 128
DIL_QUARTER = OUT_BM // DIL_RES


def _rmsnorm_rows(y, g):
    ms = jnp.mean(y * y, axis=-1, keepdims=True)
    return y * lax.rsqrt(ms + EPS) * g


def _outproj_kernel(o_ref, w_ref, x_ref, g_ref, y_ref, h_ref):
    for s in range(D_MODEL // OUT_SLAB):
        cols = slice(s * OUT_SLAB, (s + 1) * OUT_SLAB)
        y_ref[:, cols] = x_ref[:, cols] + jnp.dot(o_ref[...], w_ref[:, cols], preferred_element_type=f32)
    h_ref[...] = _rmsnorm_rows(y_ref[...], g_ref[...]).astype(bf16)


def _dil_outproj_kernel(o_ref, w_ref, x_ref, g_ref, y_ref, h_ref, nat_ref):
    o = o_ref[0].reshape(OUT_BM, D_MODEL)
    per = MXU_N // LANES
    for s in range(D_MODEL // MXU_N):
        mix = jnp.dot(o, w_ref[:, s * MXU_N:(s + 1) * MXU_N], preferred_element_type=f32)
        for u in range(per):
            c = s * per + u
            for rho in range(DIL_RES):
                nat_ref[c, pl.ds(rho, DIL_QUARTER, stride=DIL_RES), :] = (
                    mix[rho * DIL_QUARTER:(rho + 1) * DIL_QUARTER, u * LANES:(u + 1) * LANES])
            cols = slice(c * LANES, (c + 1) * LANES)
            y_ref[:, cols] = x_ref[:, cols] + nat_ref[c]
    h_ref[...] = _rmsnorm_rows(y_ref[...], g_ref[...]).astype(bf16)


def outproj_residual(o, w, x2d, g, residue_major):
    row = pl.BlockSpec((OUT_BM, D_MODEL), lambda i: (i, 0))
    if residue_major:
        quarters = DIL_TILE // OUT_BM
        o = o.reshape(TOKENS // DIL_TILE, DIL_RES, DIL_ROWS, D_MODEL)
        ospec = pl.BlockSpec((1, DIL_RES, DIL_QUARTER, D_MODEL), lambda i: (i // quarters, 0, i % quarters, 0))
        body, scratch = _dil_outproj_kernel, [pltpu.VMEM((D_MODEL // LANES, OUT_BM, LANES), f32)]
    else:
        ospec, body, scratch = row, _outproj_kernel, []
    return pl.pallas_call(
        body,
        out_shape=(jax.ShapeDtypeStruct((TOKENS, D_MODEL), f32), jax.ShapeDtypeStruct((TOKENS, D_MODEL), bf16)),
        grid=(TOKENS // OUT_BM,),
        in_specs=[ospec,
                  pl.BlockSpec((D_MODEL, D_MODEL), lambda i: (0, 0)),
                  row,
                  pl.BlockSpec((1, D_MODEL), lambda i: (0, 0))],
        out_specs=(row, row),
        scratch_shapes=scratch,
        compiler_params=_params(("arbitrary",), 52),
        name="dil_outproj_residual" if residue_major else "outproj_residual",
    )(o, w, x2d, g.reshape(1, D_MODEL))


MLP_BM = OUT_BM
MLP_BF = 1024


def _mlp_kernel(h_ref, wu_ref, wd_ref, x_ref, *refs, next_h):
    if next_h is None:
        (y_ref,) = refs
    else:
        g_ref, y_ref, hn_ref, slab_ref = refs
    f = pl.program_id(1)

    @pl.when(f == 0)
    def _():
        y_ref[...] = x_ref[...]

    a = jnp.maximum(jnp.dot(h_ref[...], wu_ref[...], preferred_element_type=f32), 0.0)
    y_ref[...] += jnp.dot((a * a).astype(bf16), wd_ref[...], preferred_element_type=f32)

    if next_h is not None:
        @pl.when(f == pl.num_programs(1) - 1)
        def _():
            hn = _rmsnorm_rows(y_ref[...], g_ref[...])
            for c in range(D_MODEL // LANES):
                slab_ref[c] = hn[:, c * LANES:(c + 1) * LANES]
            for rho in range(DIL_RES):
                for c in range(D_MODEL // LANES):
                    hn_ref[0, rho, :, c * LANES:(c + 1) * LANES] = (
                        slab_ref[c, pl.ds(rho, DIL_QUARTER, stride=DIL_RES), :].astype(bf16))


def mlp_residual(h2d, w_up, w_down, x2d, g_next=None):
    row = pl.BlockSpec((MLP_BM, D_MODEL), lambda i, f: (i, 0))
    in_specs = [row,
                pl.BlockSpec((D_MODEL, MLP_BF), lambda i, f: (0, f)),
                pl.BlockSpec((MLP_BF, D_MODEL), lambda i, f: (f, 0)),
                row]
    y_shape = jax.ShapeDtypeStruct((TOKENS, D_MODEL), f32)
    if g_next is None:
        args, out_shape, out_specs, scratch = (h2d, w_up, w_down, x2d), y_shape, row, []
    else:
        quarters = DIL_TILE // MLP_BM
        in_specs.append(pl.BlockSpec((1, D_MODEL), lambda i, f: (0, 0)))
        args = (h2d, w_up, w_down, x2d, g_next.reshape(1, D_MODEL))
        out_shape = (y_shape, jax.ShapeDtypeStruct((TOKENS // DIL_TILE, DIL_RES, DIL_ROWS, D_MODEL), bf16))
        out_specs = (row, pl.BlockSpec((1, DIL_RES, DIL_QUARTER, D_MODEL),
                                       lambda i, f: (i // quarters, 0, i % quarters, 0)))
        scratch = [pltpu.VMEM((D_MODEL // LANES, MLP_BM, LANES), f32)]
    return pl.pallas_call(
        functools.partial(_mlp_kernel, next_h=None if g_next is None else "residue_major"),
        out_shape=out_shape,
        grid=(TOKENS // MLP_BM, D_FF // MLP_BF),
        in_specs=in_specs,
        out_specs=out_specs,
        scratch_shapes=scratch,
        compiler_params=_params(("arbitrary", "arbitrary"), 56),
        name="mlp_residual",
    )(*args)


DIL_COLS = 2 * N_GROUPS * DIL_HEADS * HEAD_DIM + DIL_HEADS * DIL_V_DIM
DIL_QK_BLOCKS = 2 * N_GROUPS


def _dil_proj_kernel(h_ref, w_ref, g_ref, o_ref):
    j = pl.program_id(1)
    h = h_ref[...]

    def slabs(normalise):
        for s in range(PROJ_BN // MXU_N):
            y2 = jnp.dot(h, w_ref[:, s * MXU_N:(s + 1) * MXU_N], preferred_element_type=f32)
            if normalise:
                for u in range(MXU_N // HEAD_DIM):
                    cols = slice(s * MXU_N + u * HEAD_DIM, s * MXU_N + (u + 1) * HEAD_DIM)
                    o_ref[:, cols] = _head_rmsnorm(y2[:, u * HEAD_DIM:(u + 1) * HEAD_DIM], g_ref[0]).astype(bf16)
            else:
                o_ref[:, s * MXU_N:(s + 1) * MXU_N] = y2.astype(bf16)

    @pl.when(j < DIL_QK_BLOCKS)
    def _():
        slabs(True)

    @pl.when(j >= DIL_QK_BLOCKS)
    def _():
        slabs(False)


def dil_proj(h2d, w, gains):
    out = pl.pallas_call(
        _dil_proj_kernel,
        out_shape=jax.ShapeDtypeStruct((TOKENS, DIL_COLS), bf16),
        grid=(TOKENS // PROJ_BM, DIL_COLS // PROJ_BN),
        in_specs=[pl.BlockSpec((PROJ_BM, D_MODEL), lambda i, j: (i, 0)),
                  pl.BlockSpec((D_MODEL, PROJ_BN), lambda i, j: (0, j)),
                  pl.BlockSpec((1, 1, HEAD_DIM), lambda i, j: (j, 0, 0))],
        out_specs=pl.BlockSpec((PROJ_BM, PROJ_BN), lambda i, j: (i, j)),
        compiler_params=_params(("arbitrary", "arbitrary"), 48),
        name="dil_proj",
    )(h2d, w, gains)
    return out.reshape(BATCH, SEQ, DIL_COLS)


DIL_LEAD = 2


def _dil_distance_tables():
    def table(delta, prev_half, r):
        valid = (delta >= 0) & (delta <= DIL_SPAN)
        d = np.where(valid, delta * (r * LOG2E), MASKED).astype(np.float32)
        first = np.where(prev_half, np.float32(MASKED), d).astype(np.float32)
        return d, first

    i = np.arange(128)[:, None]
    j = np.arange(256)[None, :]
    t16 = table(128 + i - j, j < 128, 16)
    c, u = i // 32, i % 32
    cp, half, up = j // 64, (j % 64) // 32, j % 32
    t4 = table(128 + 4 * u + c - 128 * half - 4 * up - cp, half == 0, 4)
    i = np.arange(256)[:, None]
    j = np.arange(512)[None, :]
    rho, u = i // 16, i % 16
    rp, half, up = j // 32, (j % 32) // 16, j % 16
    t1 = table(256 + 16 * u + rho - 256 * half - 16 * up - rp, half == 0, 1)
    return t1, t4, t16


def _dil_attn_kernel(slopes_ref, q0_ref, q1_ref, q2_ref, k0_ref, k1_ref, k2_ref, v_ref,
                     d1_ref, d1f_ref, d4_ref, d4f_ref, d16_ref, d16f_ref,
                     o_ref, m_ref, l_ref, acc_ref):
    h = pl.program_id(1)
    t = pl.program_id(2)
    base = pl.multiple_of(t * DIL_TILE, DIL_TILE)
    pbase = pl.multiple_of(jnp.maximum(t - 1, 0) * DIL_TILE, DIL_TILE)
    first = t == 0

    def gather(ref, rows, size):
        return jnp.concatenate([ref[r:r + size, :] for r in rows], axis=0)

    def scores(u):
        q_ref, k_ref, bias, rows, size, n, _ = u
        q = jnp.concatenate([q_ref[0, r:r + size, :] for r in rows], axis=0)
        return lax.dot_general(q, window(k_ref, rows, size, n), (((1,), (1,)), ((), ())),
                               preferred_element_type=f32) + bias

    def softmax(u, s):
        _, _, _, rows, size, _, init = u
        m_new = jnp.max(s, axis=1, keepdims=True)
        m_old = None
        if not init:
            m_old = gather(m_ref, rows, size)
            m_new = jnp.maximum(m_old, m_new)
        p = jnp.exp2(s - m_new).astype(bf16)
        ones = jnp.ones((p.shape[1], LANES), bf16)
        return m_old, m_new, jnp.dot(p, ones, preferred_element_type=f32)[:, :1], p

    def accumulate(u, m_old, m_new, l_new, p):
        _, _, _, rows, size, n, init = u
        o_new = jnp.dot(p, window(v_ref, rows, size, n), preferred_element_type=f32)
        if not init:
            alpha = jnp.exp2(m_old - m_new)
            l_new = alpha * gather(l_ref, rows, size) + l_new
            o_new = alpha * gather(acc_ref, rows, size) + o_new
        for c, r in enumerate(rows):
            m_ref[r:r + size, :] = m_new[c * size:(c + 1) * size]
            l_ref[r:r + size, :] = l_new[c * size:(c + 1) * size]
            acc_ref[r:r + size, :] = o_new[c * size:(c + 1) * size]

    def window(ref, rows, size, n):
        if n > 0:
            return jnp.concatenate([ref[0, pl.ds(base + (r - size), 2 * size), :] for r in rows], axis=0)
        parts = []
        for r in rows:
            parts.append(ref[0, pl.ds(pbase + (r + DIL_ROWS - size), size), :])
            parts.append(ref[0, pl.ds(base + r, size), :])
        return jnp.concatenate(parts, axis=0)

    units = []
    slope = slopes_ref[2 * DIL_HEADS + h]
    bias = -slope * jnp.where(first, d16f_ref[...], d16_ref[...])
    for rho in range(DIL_RES):
        units.append((q2_ref, k2_ref, bias, [rho * DIL_ROWS], DIL_ROWS, 0, True))
    slope = slopes_ref[DIL_HEADS + h]
    bias_in = -slope * d4_ref[...]
    bias_edge = -slope * jnp.where(first, d4f_ref[...], d4_ref[...])
    for rho4 in range(4):
        for n in range(4):
            rows = [(rho4 + 4 * c) * DIL_ROWS + 32 * n for c in range(4)]
            units.append((q1_ref, k1_ref, bias_edge if n == 0 else bias_in, rows, 32, n, False))
    slope = slopes_ref[h]
    bias_in = -slope * d1_ref[...]
    bias_edge = -slope * jnp.where(first, d1f_ref[...], d1_ref[...])
    for n in range(DIL_ROWS // 16):
        rows = [rho * DIL_ROWS + 16 * n for rho in range(DIL_RES)]
        units.append((q0_ref, k0_ref, bias_edge if n == 0 else bias_in, rows, 16, n, False))

    s_val = {i: scores(units[i]) for i in range(DIL_LEAD)}
    soft = {}
    for x in range(len(units) + 1):
        if x + DIL_LEAD < len(units):
            s_val[x + DIL_LEAD] = scores(units[x + DIL_LEAD])
        if x >= 1:
            accumulate(units[x - 1], *soft.pop(x - 1))
        if x < len(units):
            soft[x] = softmax(units[x], s_val.pop(x))

    o_ref[0] = (acc_ref[...] / l_ref[...]).astype(bf16)


def dil_attention(p3, slopes):
    tables = [jnp.asarray(a) for pair in _dil_distance_tables() for a in pair]
    gh = DIL_HEADS
    kcol0 = N_GROUPS * gh
    vcol0 = 2 * N_GROUPS * gh * HEAD_DIM // DIL_V_DIM
    qspecs = [pl.BlockSpec((1, DIL_TILE, HEAD_DIM), lambda b, h, t, g=g: (b, t, g * gh + h))
              for g in range(N_GROUPS)]
    kspecs = [pl.BlockSpec((1, SEQ, HEAD_DIM), lambda b, h, t, g=g: (b, 0, kcol0 + g * gh + h))
              for g in range(N_GROUPS)]
    vspec = pl.BlockSpec((1, SEQ, DIL_V_DIM), lambda b, h, t: (b, 0, vcol0 + h))
    tspecs = [pl.BlockSpec(a.shape, lambda b, h, t: (0, 0)) for a in tables]
    return pl.pallas_call(
        _dil_attn_kernel,
        out_shape=jax.ShapeDtypeStruct((BATCH, SEQ, D_MODEL), bf16),
        grid=(BATCH, DIL_HEADS, SEQ // DIL_TILE),
        in_specs=[pl.BlockSpec(memory_space=pltpu.SMEM)] + qspecs + kspecs + [vspec] + tspecs,
        out_specs=pl.BlockSpec((1, DIL_TILE, DIL_V_DIM), lambda b, h, t: (b, t, h)),
        scratch_shapes=[pltpu.VMEM((DIL_TILE, 1), f32),
                        pltpu.VMEM((DIL_TILE, 1), f32),
                        pltpu.VMEM((DIL_TILE, DIL_V_DIM), f32)],
        compiler_params=_params(("arbitrary",) * 3, 48),
        name="dil_attention",
    )(slopes, p3, p3, p3, p3, p3, p3, p3, *tables)


def _fox_mixer(x2d, norm_g, w_in, b_f, q_gain, k_gain):
    wb, wf = to_bf16_split(w_in, 0, 3 * FOX_HEADS * HEAD_DIM)
    bfp = jnp.pad(b_f.astype(f32), (0, HEAD_DIM - FOX_HEADS)).reshape(1, HEAD_DIM)
    h3, *c_pieces = fox_norm_gate(x2d.reshape(BATCH, SEQ, D_MODEL), norm_g, wf, bfp)

    nb = D_MODEL // PROJ_BN
    qg = (q_gain * (QK_SCALE * LOG2E)).astype(f32).reshape(1, HEAD_DIM)
    kg = k_gain.astype(f32).reshape(1, HEAD_DIM)
    qT = fox_proj(h3, wb, qg, "q", 0)
    k = fox_proj(h3, wb, kg, "k", nb)
    vT = fox_proj(h3, wb, kg, "v", 2 * nb)
    return fox_attention(qT, c_pieces, k, vT).reshape(TOKENS, D_MODEL)


def _dil_mixer(h_rm, wb, q_gain, k_gain):
    ones = jnp.ones((DIL_COLS // PROJ_BN - DIL_QK_BLOCKS, HEAD_DIM), f32)
    gains = jnp.concatenate([q_gain.astype(f32) * (QK_SCALE * LOG2E), k_gain.astype(f32), ones], axis=0)
    p3 = dil_proj(h_rm, wb, gains.reshape(-1, 1, HEAD_DIM))
    n_heads = N_GROUPS * DIL_HEADS
    slopes = jnp.exp2(-ALIBI_MAX_EXP * jnp.arange(1, n_heads + 1, dtype=f32) / n_heads)
    return dil_attention(p3, slopes)


def kernel(x, fox_w_in, fox_b_f, fox_q_gain, fox_k_gain, fox_w_out, dil_w_in, dil_q_gain, dil_k_gain,
           dil_w_out, mix_norm_g, mlp_norm_g, mlp_w_up, mlp_w_down):
    x0 = x.reshape(TOKENS, D_MODEL)
    o = _fox_mixer(x0, mix_norm_g[0], fox_w_in, fox_b_f[0], fox_q_gain[0], fox_k_gain[0])
    x1, h = outproj_residual(o, to_bf16(fox_w_out, 0), x0, mlp_norm_g[0], residue_major=False)
    x2, h_rm = mlp_residual(h, to_bf16(mlp_w_up, 0), to_bf16(mlp_w_down, 0), x1, g_next=mix_norm_g[1])
    o = _dil_mixer(h_rm.reshape(TOKENS, D_MODEL), to_bf16(dil_w_in, 0), dil_q_gain[0], dil_k_gain[0])
    x3, h = outproj_residual(o, to_bf16(dil_w_out, 0), x2, mlp_norm_g[1], residue_major=True)
    x4 = mlp_residual(h, to_bf16(mlp_w_up, 1), to_bf16(mlp_w_down, 1), x3)
    return x4.reshape(BATCH, SEQ, D_MODEL)
```

```python
import functools

import numpy as np
import jax
import jax.numpy as jnp
from jax import lax
from jax.experimental import pallas as pl
from jax.experimental.pallas import tpu as pltpu

f32 = jnp.float32
bf16 = jnp.bfloat16

BATCH = 2
SEQ = 8192
D_MODEL = 2048
TOKENS = BATCH * SEQ
HEAD_DIM = 128
EPS = 1e-6
D_FF = 4 * D_MODEL
FOX_HEADS = D_MODEL // HEAD_DIM
DIL_PATTERNS = ((128, 1), (512, 4), (2048, 16))
N_GROUPS = len(DIL_PATTERNS)
DIL_SPAN = 128
DIL_HEADS = D_MODEL // (2 * HEAD_DIM)
DIL_V_DIM = D_MODEL // DIL_HEADS
ALIBI_MAX_EXP = 8.0

LOG2E = 1.4426950408889634
QK_SCALE = HEAD_DIM ** -0.5
MASKED = 1e30
MIB = 1024 * 1024

DIL_TILE = 2048
DIL_RES = 16
DIL_ROWS = DIL_TILE // DIL_RES


def _params(semantics, vmem_mib):
    return pltpu.CompilerParams(dimension_semantics=semantics, vmem_limit_bytes=vmem_mib * MIB)


def _cast_kernel(w_ref, o_ref):
    o_ref[...] = w_ref[...].astype(o_ref.dtype)


def to_bf16(w3, layer):
    _, rows, cols = w3.shape
    br = 256 if cols > 4096 else 1024
    return pl.pallas_call(
        _cast_kernel,
        out_shape=jax.ShapeDtypeStruct((rows, cols), bf16),
        grid=(rows // br,),
        in_specs=[pl.BlockSpec((None, br, cols), lambda i: (layer, i, 0))],
        out_specs=pl.BlockSpec((br, cols), lambda i: (i, 0)),
        compiler_params=_params(("arbitrary",), 40),
        name="to_bf16",
    )(w3)


def _cast_transpose_kernel(wt_ref, o_ref):
    o_ref[...] = wt_ref[...].T.astype(o_ref.dtype)


def _cast_tail_kernel(wt_ref, t_ref):
    t_ref[...] = jnp.zeros_like(t_ref)
    t_ref[:wt_ref.shape[0], :] = wt_ref[...].astype(t_ref.dtype)


def to_bf16_split(w3, layer, split):
    _, rows, cols = w3.shape
    tail = cols - split
    wt = jnp.swapaxes(w3, 1, 2)
    bc = 256
    main = pl.pallas_call(
        _cast_transpose_kernel,
        out_shape=jax.ShapeDtypeStruct((rows, split), bf16),
        grid=(split // bc,),
        in_specs=[pl.BlockSpec((None, bc, rows), lambda i: (layer, i, 0))],
        out_specs=pl.BlockSpec((rows, bc), lambda i: (0, i)),
        compiler_params=_params(("arbitrary",), 40),
        name="to_bf16_transposed",
    )(wt)
    rest = pl.pallas_call(
        _cast_tail_kernel,
        out_shape=jax.ShapeDtypeStruct((HEAD_DIM, rows), bf16),
        grid=(1,),
        in_specs=[pl.BlockSpec((None, tail, rows), lambda i: (layer, split // tail, 0))],
        out_specs=pl.BlockSpec((HEAD_DIM, rows), lambda i: (0, 0)),
        compiler_params=_params(("arbitrary",), 40),
        name="to_bf16_tail",
    )(wt)
    return main, rest


def _head_rmsnorm(y, gain):
    ms = jnp.mean(y * y, axis=-1, keepdims=True)
    return y * lax.rsqrt(ms + EPS) * gain


GATE_BM = 512


def _gate_kernel(x_ref, g_ref, wf_ref, bf_ref, h_ref, hi_ref, mid_ref, lo_ref, carry_ref):
    @pl.when(pl.program_id(1) == 0)
    def _():
        carry_ref[...] = jnp.zeros_like(carry_ref)

    x = x_ref[0]
    h = (x * lax.rsqrt(jnp.mean(x * x, axis=-1, keepdims=True) + EPS) * g_ref[...]).astype(bf16)
    h_ref[0] = h
    f = lax.dot_general(h, wf_ref[...], (((1,), (1,)), ((), ())), preferred_element_type=f32) + bf_ref[...]
    lf = jnp.minimum(f, 0.0) - jnp.log1p(jnp.exp(-jnp.abs(f)))
    row = lax.broadcasted_iota(jnp.int32, (GATE_BM, GATE_BM), 0)
    col = lax.broadcasted_iota(jnp.int32, (GATE_BM, GATE_BM), 1)
    tri = (col <= row).astype(bf16)
    p0 = lf.astype(bf16)
    r0 = lf - p0.astype(f32)
    p1 = r0.astype(bf16)
    p2 = (r0 - p1.astype(f32)).astype(bf16)
    cs = (jnp.dot(tri, p0, preferred_element_type=f32)
          + jnp.dot(tri, p1, preferred_element_type=f32)
          + jnp.dot(tri, p2, preferred_element_type=f32))
    c = cs + carry_ref[...]
    carry_ref[...] = c[GATE_BM - 1:GATE_BM, :]
    c2 = c * LOG2E
    hi = c2.astype(bf16)
    r1 = c2 - hi.astype(f32)
    mid = r1.astype(bf16)
    lo = (r1 - mid.astype(f32)).astype(bf16)
    hi_ref[0] = hi
    mid_ref[0] = mid
    lo_ref[0] = lo


def fox_norm_gate(x3, g, wf, bfp):
    piece = jax.ShapeDtypeStruct((BATCH, SEQ, HEAD_DIM), bf16)
    spec = pl.BlockSpec((1, GATE_BM, HEAD_DIM), lambda b, s: (b, s, 0))
    rows = pl.BlockSpec((1, GATE_BM, D_MODEL), lambda b, s: (b, s, 0))
    return pl.pallas_call(
        _gate_kernel,
        out_shape=(jax.ShapeDtypeStruct((BATCH, SEQ, D_MODEL), bf16), piece, piece, piece),
        grid=(BATCH, SEQ // GATE_BM),
        in_specs=[rows,
                  pl.BlockSpec((1, D_MODEL), lambda b, s: (0, 0)),
                  pl.BlockSpec((HEAD_DIM, D_MODEL), lambda b, s: (0, 0)),
                  pl.BlockSpec((1, HEAD_DIM), lambda b, s: (0, 0))],
        out_specs=(rows, spec, spec, spec),
        scratch_shapes=[pltpu.VMEM((1, HEAD_DIM), f32)],
        compiler_params=_params(("arbitrary", "arbitrary"), 32),
        name="fox_norm_gate",
    )(x3, g.reshape(1, D_MODEL), wf, bfp)


PROJ_BM = 1024
PROJ_BN = 1024
PROJ_HEADS = PROJ_BN // HEAD_DIM
MXU_N = 256
FOX_BK = 256
FOX_V_ROWS = HEAD_DIM + 16


def _for_each_slab(a, w_ref, width, consume):
    n = w_ref.shape[1] // width

    def product(s):
        return jnp.dot(a, w_ref[:, s * width:(s + 1) * width], preferred_element_type=f32)

    y = product(0)
    for s in range(n):
        y_next = product(s + 1) if s + 1 < n else None
        consume(s, y)
        y = y_next


def _fox_proj_kernel(h_ref, w_ref, g_ref, o_ref, *, mode):
    if mode == "v":
        extra = lax.broadcasted_iota(jnp.int32, (FOX_V_ROWS - HEAD_DIM, FOX_BK), 0)
        ones_row = jnp.where(extra == 0, 1.0, 0.0).astype(bf16)

    def consume(s, y2):
        for u in range(MXU_N // HEAD_DIM):
            hh = s * (MXU_N // HEAD_DIM) + u
            y = y2[:, u * HEAD_DIM:(u + 1) * HEAD_DIM]
            if mode == "q":
                o_ref[0, hh] = _head_rmsnorm(y, g_ref[...]).T.astype(bf16)
            elif mode == "k":
                o_ref[0, hh] = _head_rmsnorm(y, g_ref[...]).astype(bf16)
            else:
                for c in range(PROJ_BM // FOX_BK):
                    o_ref[0, hh, c, :HEAD_DIM, :] = y[c * FOX_BK:(c + 1) * FOX_BK, :].T.astype(bf16)
                    o_ref[0, hh, c, HEAD_DIM:, :] = ones_row

    _for_each_slab(h_ref[0], w_ref, MXU_N, consume)


def fox_proj(h3, w, gain, mode, col_block0):
    nsb = SEQ // PROJ_BM
    if mode == "q":
        shape = (BATCH, FOX_HEADS, HEAD_DIM, SEQ)
        ospec = pl.BlockSpec((1, PROJ_HEADS, HEAD_DIM, PROJ_BM), lambda b, s, j: (b, j, 0, s))
    elif mode == "k":
        shape = (BATCH, FOX_HEADS, SEQ, HEAD_DIM)
        ospec = pl.BlockSpec((1, PROJ_HEADS, PROJ_BM, HEAD_DIM), lambda b, s, j: (b, j, s, 0))
    else:
        shape = (BATCH, FOX_HEADS, SEQ // FOX_BK, FOX_V_ROWS, FOX_BK)
        ospec = pl.BlockSpec((1, PROJ_HEADS, PROJ_BM // FOX_BK, FOX_V_ROWS, FOX_BK),
                             lambda b, s, j: (b, j, s, 0, 0))
    return pl.pallas_call(
        functools.partial(_fox_proj_kernel, mode=mode),
        out_shape=jax.ShapeDtypeStruct(shape, bf16),
        grid=(BATCH, nsb, D_MODEL // PROJ_BN),
        in_specs=[pl.BlockSpec((1, PROJ_BM, D_MODEL), lambda b, s, j: (b, s, 0)),
                  pl.BlockSpec((D_MODEL, PROJ_BN), lambda b, s, j: (0, col_block0 + j)),
                  pl.BlockSpec((1, HEAD_DIM), lambda b, s, j: (0, 0))],
        out_specs=ospec,
        compiler_params=_params(("arbitrary",) * 3, 48),
        name="fox_proj_" + mode,
    )(h3, w, gain)


FOX_BQ = 2048
FOX_TILES = FOX_BQ // FOX_BK
FOX_LEAD = 4
AUG = 2 * HEAD_DIM
AUG_ROWS = 16
AUG_CHUNK = 1024


def _fox_attn_kernel(qT_ref, chi_ref, cmid_ref, clo_ref, k_ref, vT_ref, o_ref,
                     kaug_ref, s_ref, m_ref, acc_ref):
    h = pl.program_id(1)
    qi = pl.program_id(2)
    pieces = (chi_ref, cmid_ref, clo_ref)
    n_p = len(pieces)

    @pl.when(qi == 0)
    def _():
        kaug_ref[:, :HEAD_DIM] = k_ref[0, 0]
        row = lax.broadcasted_iota(jnp.int32, (HEAD_DIM, HEAD_DIM), 0)
        col = lax.broadcasted_iota(jnp.int32, (HEAD_DIM, HEAD_DIM), 1)
        lane = lax.broadcasted_iota(jnp.int32, (AUG_CHUNK, HEAD_DIM), 1)
        for c in range(SEQ // AUG_CHUNK):
            rows = slice(c * AUG_CHUNK, (c + 1) * AUG_CHUNK)
            aug = jnp.where(lane < n_p, 1.0, 0.0)
            for p, ref in enumerate(pieces):
                sel = jnp.where((row == h) & (col == n_p + p), -1.0, 0.0).astype(bf16)
                aug = aug + jnp.dot(ref[0, rows, :], sel, preferred_element_type=f32)
            kaug_ref[rows, HEAD_DIM:] = aug.astype(bf16)

    q0 = pl.multiple_of(qi * FOX_BQ, FOX_BQ)
    row = lax.broadcasted_iota(jnp.int32, (AUG_ROWS, HEAD_DIM), 0)
    col = lax.broadcasted_iota(jnp.int32, (AUG_ROWS, HEAD_DIM), 1)
    slot = lax.broadcasted_iota(jnp.int32, (AUG_ROWS, FOX_BQ), 0)
    qa = jnp.where((slot >= n_p) & (slot < 2 * n_p), 1.0, 0.0)
    for p, ref in enumerate(pieces):
        sel = jnp.where((row == p) & (col == h), 1.0, 0.0).astype(bf16)
        qa = qa + lax.dot_general(sel, ref[0, pl.ds(q0, FOX_BQ), :], (((1,), (1,)), ((), ())),
                                  preferred_element_type=f32)
    q_aug = jnp.concatenate([qT_ref[0, 0], qa.astype(bf16),
                             jnp.zeros((HEAD_DIM - AUG_ROWS, FOX_BQ), bf16)], axis=0)

    def scores(kb, q):
        start = pl.multiple_of(kb * FOX_BK, FOX_BK)
        return jnp.dot(kaug_ref[pl.ds(start, FOX_BK), :], q, preferred_element_type=f32)

    def softmax_step(s, m):
        m_new = jnp.maximum(m, jnp.max(s, axis=0, keepdims=True))
        return m_new, jnp.exp2(m - m_new), jnp.exp2(s - m_new).astype(bf16)

    def pv(kb, p):
        return jnp.dot(vT_ref[0, 0, kb], p, preferred_element_type=f32)

    n_slabs = FOX_BQ // MXU_N
    slabs = [slice(n * MXU_N, (n + 1) * MXU_N) for n in range(n_slabs)]
    q_slabs = [q_aug[:, c] for c in slabs]
    kk = lax.broadcasted_iota(jnp.int32, (FOX_BK, MXU_N), 0)
    qq = lax.broadcasted_iota(jnp.int32, (FOX_BK, MXU_N), 1)

    def run_units(units, lookahead):
        m = [m_ref[:, c] for c in slabs]
        acc = [acc_ref[:, c] for c in slabs]
        todo = units + lookahead
        s_val = {i: s_ref[:, slabs[i]] for i in range(FOX_LEAD)}
        soft = {}
        for x in range(len(units) + 1):
            if x + FOX_LEAD < len(todo):
                kb, n, _ = todo[x + FOX_LEAD]
                s_val[x + FOX_LEAD] = scores(kb, q_slabs[n])
            if x >= 1:
                kb, n, _ = units[x - 1]
                alpha, p = soft.pop(x - 1)
                acc[n] = alpha * acc[n] + pv(kb, p)
            if x < len(units):
                kb, n, diagonal = units[x]
                s = s_val.pop(x)
                if diagonal:
                    s = jnp.where(kk <= qq, s, -MASKED)
                m[n], alpha, p = softmax_step(s, m[n])
                soft[x] = (alpha, p)
        for i in range(len(lookahead)):
            s_ref[:, slabs[i]] = s_val[len(units) + i]
        for n, c in enumerate(slabs):
            m_ref[:, c], acc_ref[:, c] = m[n], acc[n]

    for i in range(FOX_LEAD):
        s_ref[:, slabs[i]] = scores(0, q_slabs[i])
    m_ref[...] = jnp.full_like(m_ref, -MASKED)
    acc_ref[...] = jnp.zeros_like(acc_ref)

    def body(j, carry):
        kb0 = j * FOX_TILES
        run_units([(kb0 + u, n, False) for u in range(FOX_TILES) for n in range(n_slabs)],
                  [(kb0 + FOX_TILES, n, False) for n in range(FOX_LEAD)])
        return carry

    lax.fori_loop(0, qi, body, 0)

    kb0 = qi * FOX_TILES
    run_units([(kb0 + d, n, n == d) for d in range(FOX_TILES) for n in range(d, n_slabs)], [])
    o = acc_ref[:HEAD_DIM, :] / acc_ref[HEAD_DIM:HEAD_DIM + 1, :]
    o_ref[0] = o.T.astype(bf16)


def fox_attention(qT, c_pieces, k, vT):
    nq = SEQ // FOX_BQ
    nk = SEQ // FOX_BK
    cspec = pl.BlockSpec((1, SEQ, HEAD_DIM), lambda b, h, i: (b, 0, 0))
    return pl.pallas_call(
        _fox_attn_kernel,
        out_shape=jax.ShapeDtypeStruct((BATCH, SEQ, D_MODEL), bf16),
        grid=(BATCH, FOX_HEADS, nq),
        in_specs=[pl.BlockSpec((1, 1, HEAD_DIM, FOX_BQ), lambda b, h, i: (b, h, 0, i)),
                  cspec, cspec, cspec,
                  pl.BlockSpec((1, 1, SEQ, HEAD_DIM), lambda b, h, i: (b, h, 0, 0)),
                  pl.BlockSpec((1, 1, nk, FOX_V_ROWS, FOX_BK), lambda b, h, i: (b, h, 0, 0, 0))],
        out_specs=pl.BlockSpec((1, FOX_BQ, HEAD_DIM), lambda b, h, i: (b, i, h)),
        scratch_shapes=[pltpu.VMEM((SEQ, AUG), bf16),
                        pltpu.VMEM((FOX_BK, FOX_LEAD * MXU_N), f32),
                        pltpu.VMEM((1, FOX_BQ), f32),
                        pltpu.VMEM((FOX_V_ROWS, FOX_BQ), f32)],
        compiler_params=_params(("arbitrary",) * 3, 48),
        name="fox_attention",
    )(qT, *c_pieces, k, vT)


OUT_BM = 512
OUT_SLAB = 512
LANES = 128
DIL_QUARTER = OUT_BM // DIL_RES


def _rmsnorm_rows(y, g):
    ms = jnp.mean(y * y, axis=-1, keepdims=True)
    return y * lax.rsqrt(ms + EPS) * g


def _outproj_kernel(o_ref, w_ref, x_ref, g_ref, y_ref, h_ref):
    def consume(s, mix):
        cols = slice(s * OUT_SLAB, (s + 1) * OUT_SLAB)
        y_ref[:, cols] = x_ref[:, cols] + mix

    _for_each_slab(o_ref[...], w_ref, OUT_SLAB, consume)
    h_ref[...] = _rmsnorm_rows(y_ref[...], g_ref[...]).astype(bf16)


def _dil_outproj_kernel(o_ref, w_ref, x_ref, g_ref, y_ref, h_ref, nat_ref):
    per = MXU_N // LANES

    def consume(s, mix):
        for u in range(per):
            c = s * per + u
            for rho in range(DIL_RES):
                nat_ref[c, pl.ds(rho, DIL_QUARTER, stride=DIL_RES), :] = (
                    mix[rho * DIL_QUARTER:(rho + 1) * DIL_QUARTER, u * LANES:(u + 1) * LANES])
            cols = slice(c * LANES, (c + 1) * LANES)
            y_ref[:, cols] = x_ref[:, cols] + nat_ref[c]

    _for_each_slab(o_ref[0].reshape(OUT_BM, D_MODEL), w_ref, MXU_N, consume)
    h_ref[...] = _rmsnorm_rows(y_ref[...], g_ref[...]).astype(bf16)


def outproj_residual(o, w, x2d, g, residue_major):
    row = pl.BlockSpec((OUT_BM, D_MODEL), lambda i: (i, 0))
    if residue_major:
        quarters = DIL_TILE // OUT_BM
        o = o.reshape(TOKENS // DIL_TILE, DIL_RES, DIL_ROWS, D_MODEL)
        ospec = pl.BlockSpec((1, DIL_RES, DIL_QUARTER, D_MODEL), lambda i: (i // quarters, 0, i % quarters, 0))
        body, scratch = _dil_outproj_kernel, [pltpu.VMEM((D_MODEL // LANES, OUT_BM, LANES), f32)]
    else:
        ospec, body, scratch = row, _outproj_kernel, []
    return pl.pallas_call(
        body,
        out_shape=(jax.ShapeDtypeStruct((TOKENS, D_MODEL), f32), jax.ShapeDtypeStruct((TOKENS, D_MODEL), bf16)),
        grid=(TOKENS // OUT_BM,),
        in_specs=[ospec,
                  pl.BlockSpec((D_MODEL, D_MODEL), lambda i: (0, 0)),
                  row,
                  pl.BlockSpec((1, D_MODEL), lambda i: (0, 0))],
        out_specs=(row, row),
        scratch_shapes=scratch,
        compiler_params=_params(("arbitrary",), 52),
        name="dil_outproj_residual" if residue_major else "outproj_residual",
    )(o, w, x2d, g.reshape(1, D_MODEL))


MLP_BM = OUT_BM
MLP_BF = 1024


def _mlp_kernel(h_ref, wu_ref, wd_ref, x_ref, *refs, next_h):
    if next_h is None:
        (y_ref,) = refs
    else:
        g_ref, y_ref, hn_ref, slab_ref = refs
    f = pl.program_id(1)

    @pl.when(f == 0)
    def _():
        y_ref[...] = x_ref[...]

    a = jnp.maximum(jnp.dot(h_ref[...], wu_ref[...], preferred_element_type=f32), 0.0)
    y_ref[...] += jnp.dot((a * a).astype(bf16), wd_ref[...], preferred_element_type=f32)

    if next_h is not None:
        @pl.when(f == pl.num_programs(1) - 1)
        def _():
            hn = _rmsnorm_rows(y_ref[...], g_ref[...])
            for c in range(D_MODEL // LANES):
                slab_ref[c] = hn[:, c * LANES:(c + 1) * LANES]
            for rho in range(DIL_RES):
                for c in range(D_MODEL // LANES):
                    hn_ref[0, rho, :, c * LANES:(c + 1) * LANES] = (
                        slab_ref[c, pl.ds(rho, DIL_QUARTER, stride=DIL_RES), :].astype(bf16))


def mlp_residual(h2d, w_up, w_down, x2d, g_next=None):
    row = pl.BlockSpec((MLP_BM, D_MODEL), lambda i, f: (i, 0))
    in_specs = [row,
                pl.BlockSpec((D_MODEL, MLP_BF), lambda i, f: (0, f)),
                pl.BlockSpec((MLP_BF, D_MODEL), lambda i, f: (f, 0)),
                row]
    y_shape = jax.ShapeDtypeStruct((TOKENS, D_MODEL), f32)
    if g_next is None:
        args, out_shape, out_specs, scratch = (h2d, w_up, w_down, x2d), y_shape, row, []
    else:
        quarters = DIL_TILE // MLP_BM
        in_specs.append(pl.BlockSpec((1, D_MODEL), lambda i, f: (0, 0)))
        args = (h2d, w_up, w_down, x2d, g_next.reshape(1, D_MODEL))
        out_shape = (y_shape, jax.ShapeDtypeStruct((TOKENS // DIL_TILE, DIL_RES, DIL_ROWS, D_MODEL), bf16))
        out_specs = (row, pl.BlockSpec((1, DIL_RES, DIL_QUARTER, D_MODEL),
                                       lambda i, f: (i // quarters, 0, i % quarters, 0)))
        scratch = [pltpu.VMEM((D_MODEL // LANES, MLP_BM, LANES), f32)]
    return pl.pallas_call(
        functools.partial(_mlp_kernel, next_h=None if g_next is None else "residue_major"),
        out_shape=out_shape,
        grid=(TOKENS // MLP_BM, D_FF // MLP_BF),
        in_specs=in_specs,
        out_specs=out_specs,
        scratch_shapes=scratch,
        compiler_params=_params(("arbitrary", "arbitrary"), 56),
        name="mlp_residual",
    )(*args)


DIL_COLS = 2 * N_GROUPS * DIL_HEADS * HEAD_DIM + DIL_HEADS * DIL_V_DIM
DIL_QK_BLOCKS = 2 * N_GROUPS


def _dil_proj_kernel(h_ref, w_ref, g_ref, o_ref):
    j = pl.program_id(1)

    def slabs(normalise):
        def consume(s, y2):
            if normalise:
                for u in range(MXU_N // HEAD_DIM):
                    cols = slice(s * MXU_N + u * HEAD_DIM, s * MXU_N + (u + 1) * HEAD_DIM)
                    o_ref[:, cols] = _head_rmsnorm(y2[:, u * HEAD_DIM:(u + 1) * HEAD_DIM], g_ref[0]).astype(bf16)
            else:
                o_ref[:, s * MXU_N:(s + 1) * MXU_N] = y2.astype(bf16)

        _for_each_slab(h_ref[...], w_ref, MXU_N, consume)

    @pl.when(j < DIL_QK_BLOCKS)
    def _():
        slabs(True)

    @pl.when(j >= DIL_QK_BLOCKS)
    def _():
        slabs(False)


def dil_proj(h2d, w, gains):
    out = pl.pallas_call(
        _dil_proj_kernel,
        out_shape=jax.ShapeDtypeStruct((TOKENS, DIL_COLS), bf16),
        grid=(TOKENS // PROJ_BM, DIL_COLS // PROJ_BN),
        in_specs=[pl.BlockSpec((PROJ_BM, D_MODEL), lambda i, j: (i, 0)),
                  pl.BlockSpec((D_MODEL, PROJ_BN), lambda i, j: (0, j)),
                  pl.BlockSpec((1, 1, HEAD_DIM), lambda i, j: (j, 0, 0))],
        out_specs=pl.BlockSpec((PROJ_BM, PROJ_BN), lambda i, j: (i, j)),
        compiler_params=_params(("arbitrary", "arbitrary"), 48),
        name="dil_proj",
    )(h2d, w, gains)
    return out.reshape(BATCH, SEQ, DIL_COLS)


DIL_LEAD = 2


def _dil_distance_tables():
    def table(delta, prev_half, r):
        valid = (delta >= 0) & (delta <= DIL_SPAN)
        d = np.where(valid, delta * (r * LOG2E), MASKED).astype(np.float32)
        first = np.where(prev_half, np.float32(MASKED), d).astype(np.float32)
        return d, first

    i = np.arange(128)[:, None]
    j = np.arange(256)[None, :]
    t16 = table(128 + i - j, j < 128, 16)
    c, u = i // 32, i % 32
    cp, half, up = j // 64, (j % 64) // 32, j % 32
    t4 = table(128 + 4 * u + c - 128 * half - 4 * up - cp, half == 0, 4)
    i = np.arange(256)[:, None]
    j = np.arange(512)[None, :]
    rho, u = i // 16, i % 16
    rp, half, up = j // 32, (j % 32) // 16, j % 16
    t1 = table(256 + 16 * u + rho - 256 * half - 16 * up - rp, half == 0, 1)
    return t1, t4, t16


def _dil_attn_kernel(slopes_ref, q0_ref, q1_ref, q2_ref, k0_ref, k1_ref, k2_ref, v_ref,
                     d1_ref, d1f_ref, d4_ref, d4f_ref, d16_ref, d16f_ref,
                     o_ref, m_ref, l_ref, acc_ref):
    h = pl.program_id(1)
    t = pl.program_id(2)
    base = pl.multiple_of(t * DIL_TILE, DIL_TILE)
    pbase = pl.multiple_of(jnp.maximum(t - 1, 0) * DIL_TILE, DIL_TILE)
    first = t == 0

    def gather(ref, rows, size):
        return jnp.concatenate([ref[r:r + size, :] for r in rows], axis=0)

    def scores(u):
        q_ref, k_ref, bias, rows, size, n, _ = u
        q = jnp.concatenate([q_ref[0, r:r + size, :] for r in rows], axis=0)
        return lax.dot_general(q, window(k_ref, rows, size, n), (((1,), (1,)), ((), ())),
                               preferred_element_type=f32) + bias

    def softmax(u, s):
        _, _, _, rows, size, _, init = u
        m_new = jnp.max(s, axis=1, keepdims=True)
        m_old = None
        if not init:
            m_old = gather(m_ref, rows, size)
            m_new = jnp.maximum(m_old, m_new)
        p = jnp.exp2(s - m_new).astype(bf16)
        ones = jnp.ones((p.shape[1], LANES), bf16)
        return m_old, m_new, jnp.dot(p, ones, preferred_element_type=f32)[:, :1], p

    def accumulate(u, m_old, m_new, l_new, p):
        _, _, _, rows, size, n, init = u
        o_new = jnp.dot(p, window(v_ref, rows, size, n), preferred_element_type=f32)
        if not init:
            alpha = jnp.exp2(m_old - m_new)
            l_new = alpha * gather(l_ref, rows, size) + l_new
            o_new = alpha * gather(acc_ref, rows, size) + o_new
        for c, r in enumerate(rows):
            m_ref[r:r + size, :] = m_new[c * size:(c + 1) * size]
            l_ref[r:r + size, :] = l_new[c * size:(c + 1) * size]
            acc_ref[r:r + size, :] = o_new[c * size:(c + 1) * size]

    def window(ref, rows, size, n):
        if n > 0:
            return jnp.concatenate([ref[0, pl.ds(base + (r - size), 2 * size), :] for r in rows], axis=0)
        parts = []
        for r in rows:
            parts.append(ref[0, pl.ds(pbase + (r + DIL_ROWS - size), size), :])
            parts.append(ref[0, pl.ds(base + r, size), :])
        return jnp.concatenate(parts, axis=0)

    units = []
    slope = slopes_ref[2 * DIL_HEADS + h]
    bias = -slope * jnp.where(first, d16f_ref[...], d16_ref[...])
    for rho in range(DIL_RES):
        units.append((q2_ref, k2_ref, bias, [rho * DIL_ROWS], DIL_ROWS, 0, True))
    slope = slopes_ref[DIL_HEADS + h]
    bias_in = -slope * d4_ref[...]
    bias_edge = -slope * jnp.where(first, d4f_ref[...], d4_ref[...])
    for rho4 in range(4):
        for n in range(4):
            rows = [(rho4 + 4 * c) * DIL_ROWS + 32 * n for c in range(4)]
            units.append((q1_ref, k1_ref, bias_edge if n == 0 else bias_in, rows, 32, n, False))
    slope = slopes_ref[h]
    bias_in = -slope * d1_ref[...]
    bias_edge = -slope * jnp.where(first, d1f_ref[...], d1_ref[...])
    for n in range(DIL_ROWS // 16):
        rows = [rho * DIL_ROWS + 16 * n for rho in range(DIL_RES)]
        units.append((q0_ref, k0_ref, bias_edge if n == 0 else bias_in, rows, 16, n, False))

    s_val = {i: scores(units[i]) for i in range(DIL_LEAD)}
    soft = {}
    for x in range(len(units) + 1):
        if x + DIL_LEAD < len(units):
            s_val[x + DIL_LEAD] = scores(units[x + DIL_LEAD])
        if x >= 1:
            accumulate(units[x - 1], *soft.pop(x - 1))
        if x < len(units):
            soft[x] = softmax(units[x], s_val.pop(x))

    o_ref[0] = (acc_ref[...] / l_ref[...]).astype(bf16)


def dil_attention(p3, slopes):
    tables = [jnp.asarray(a) for pair in _dil_distance_tables() for a in pair]
    gh = DIL_HEADS
    kcol0 = N_GROUPS * gh
    vcol0 = 2 * N_GROUPS * gh * HEAD_DIM // DIL_V_DIM
    qspecs = [pl.BlockSpec((1, DIL_TILE, HEAD_DIM), lambda b, h, t, g=g: (b, t, g * gh + h))
              for g in range(N_GROUPS)]
    kspecs = [pl.BlockSpec((1, SEQ, HEAD_DIM), lambda b, h, t, g=g: (b, 0, kcol0 + g * gh + h))
              for g in range(N_GROUPS)]
    vspec = pl.BlockSpec((1, SEQ, DIL_V_DIM), lambda b, h, t: (b, 0, vcol0 + h))
    tspecs = [pl.BlockSpec(a.shape, lambda b, h, t: (0, 0)) for a in tables]
    return pl.pallas_call(
        _dil_attn_kernel,
        out_shape=jax.ShapeDtypeStruct((BATCH, SEQ, D_MODEL), bf16),
        grid=(BATCH, DIL_HEADS, SEQ // DIL_TILE),
        in_specs=[pl.BlockSpec(memory_space=pltpu.SMEM)] + qspecs + kspecs + [vspec] + tspecs,
        out_specs=pl.BlockSpec((1, DIL_TILE, DIL_V_DIM), lambda b, h, t: (b, t, h)),
        scratch_shapes=[pltpu.VMEM((DIL_TILE, 1), f32),
                        pltpu.VMEM((DIL_TILE, 1), f32),
                        pltpu.VMEM((DIL_TILE, DIL_V_DIM), f32)],
        compiler_params=_params(("arbitrary",) * 3, 48),
        name="dil_attention",
    )(slopes, p3, p3, p3, p3, p3, p3, p3, *tables)


def _fox_mixer(x2d, norm_g, w_in, b_f, q_gain, k_gain):
    wb, wf = to_bf16_split(w_in, 0, 3 * FOX_HEADS * HEAD_DIM)
    bfp = jnp.pad(b_f.astype(f32), (0, HEAD_DIM - FOX_HEADS)).reshape(1, HEAD_DIM)
    h3, *c_pieces = fox_norm_gate(x2d.reshape(BATCH, SEQ, D_MODEL), norm_g, wf, bfp)

    nb = D_MODEL // PROJ_BN
    qg = (q_gain * (QK_SCALE * LOG2E)).astype(f32).reshape(1, HEAD_DIM)
    kg = k_gain.astype(f32).reshape(1, HEAD_DIM)
    qT = fox_proj(h3, wb, qg, "q", 0)
    k = fox_proj(h3, wb, kg, "k", nb)
    vT = fox_proj(h3, wb, kg, "v", 2 * nb)
    return fox_attention(qT, c_pieces, k, vT).reshape(TOKENS, D_MODEL)


def _dil_mixer(h_rm, wb, q_gain, k_gain):
    ones = jnp.ones((DIL_COLS // PROJ_BN - DIL_QK_BLOCKS, HEAD_DIM), f32)
    gains = jnp.concatenate([q_gain.astype(f32) * (QK_SCALE * LOG2E), k_gain.astype(f32), ones], axis=0)
    p3 = dil_proj(h_rm, wb, gains.reshape(-1, 1, HEAD_DIM))
    n_heads = N_GROUPS * DIL_HEADS
    slopes = jnp.exp2(-ALIBI_MAX_EXP * jnp.arange(1, n_heads + 1, dtype=f32) / n_heads)
    return dil_attention(p3, slopes)


def kernel(x, fox_w_in, fox_b_f, fox_q_gain, fox_k_gain, fox_w_out, dil_w_in, dil_q_gain, dil_k_gain,
           dil_w_out, mix_norm_g, mlp_norm_g, mlp_w_up, mlp_w_down):
    x0 = x.reshape(TOKENS, D_MODEL)
    o = _fox_mixer(x0, mix_norm_g[0], fox_w_in, fox_b_f[0], fox_q_gain[0], fox_k_gain[0])
    x1, h = outproj_residual(o, to_bf16(fox_w_out, 0), x0, mlp_norm_g[0], residue_major=False)
    x2, h_rm = mlp_residual(h, to_bf16(mlp_w_up, 0), to_bf16(mlp_w_down, 0), x1, g_next=mix_norm_g[1])
    o = _dil_mixer(h_rm.reshape(TOKENS, D_MODEL), to_bf16(dil_w_in, 0), dil_q_gain[0], dil_k_gain[0])
    x3, h = outproj_residual(o, to_bf16(dil_w_out, 0), x2, mlp_norm_g[1], residue_major=True)
    x4 = mlp_residual(h, to_bf16(mlp_w_up, 1), to_bf16(mlp_w_down, 1), x3)
    return x4.reshape(BATCH, SEQ, D_MODEL)
```

```python
import functools

import numpy as np
import jax
import jax.numpy as jnp
from jax import lax
from jax.experimental import pallas as pl
from jax.experimental.pallas import tpu as pltpu

f32 = jnp.float32
bf16 = jnp.bfloat16

BATCH = 2
SEQ = 8192
D_MODEL = 2048
TOKENS = BATCH * SEQ
HEAD_DIM = 128
EPS = 1e-6
D_FF = 4 * D_MODEL
FOX_HEADS = D_MODEL // HEAD_DIM
DIL_PATTERNS = ((128, 1), (512, 4), (2048, 16))
N_GROUPS = len(DIL_PATTERNS)
DIL_SPAN = 128
DIL_HEADS = D_MODEL // (2 * HEAD_DIM)
DIL_V_DIM = D_MODEL // DIL_HEADS
ALIBI_MAX_EXP = 8.0

LOG2E = 1.4426950408889634
QK_SCALE = HEAD_DIM ** -0.5
MASKED = 1e30
MIB = 1024 * 1024

DIL_TILE = 2048
DIL_RES = 16
DIL_ROWS = DIL_TILE // DIL_RES


def _params(semantics, vmem_mib):
    return pltpu.CompilerParams(dimension_semantics=semantics, vmem_limit_bytes=vmem_mib * MIB)


def _cast_kernel(w_ref, o_ref):
    o_ref[...] = w_ref[...].astype(o_ref.dtype)


def to_bf16(w3, layer):
    _, rows, cols = w3.shape
    br = 256 if cols > 4096 else 1024
    return pl.pallas_call(
        _cast_kernel,
        out_shape=jax.ShapeDtypeStruct((rows, cols), bf16),
        grid=(rows // br,),
        in_specs=[pl.BlockSpec((None, br, cols), lambda i: (layer, i, 0))],
        out_specs=pl.BlockSpec((br, cols), lambda i: (i, 0)),
        compiler_params=_params(("arbitrary",), 40),
        name="to_bf16",
    )(w3)


def _cast_transpose_kernel(wt_ref, o_ref):
    o_ref[...] = wt_ref[...].T.astype(o_ref.dtype)


def _cast_tail_kernel(wt_ref, t_ref):
    t_ref[...] = jnp.zeros_like(t_ref)
    t_ref[:wt_ref.shape[0], :] = wt_ref[...].astype(t_ref.dtype)


def to_bf16_split(w3, layer, split):
    _, rows, cols = w3.shape
    tail = cols - split
    wt = jnp.swapaxes(w3, 1, 2)
    bc = 256
    main = pl.pallas_call(
        _cast_transpose_kernel,
        out_shape=jax.ShapeDtypeStruct((rows, split), bf16),
        grid=(split // bc,),
        in_specs=[pl.BlockSpec((None, bc, rows), lambda i: (layer, i, 0))],
        out_specs=pl.BlockSpec((rows, bc), lambda i: (0, i)),
        compiler_params=_params(("arbitrary",), 40),
        name="to_bf16_transposed",
    )(wt)
    rest = pl.pallas_call(
        _cast_tail_kernel,
        out_shape=jax.ShapeDtypeStruct((HEAD_DIM, rows), bf16),
        grid=(1,),
        in_specs=[pl.BlockSpec((None, tail, rows), lambda i: (layer, split // tail, 0))],
        out_specs=pl.BlockSpec((HEAD_DIM, rows), lambda i: (0, 0)),
        compiler_params=_params(("arbitrary",), 40),
        name="to_bf16_tail",
    )(wt)
    return main, rest


def _head_rmsnorm(y, gain):
    ms = jnp.mean(y * y, axis=-1, keepdims=True)
    return y * lax.rsqrt(ms + EPS) * gain


GATE_BM = 512


def _gate_kernel(x_ref, g_ref, wf_ref, bf_ref, h_ref, hi_ref, mid_ref, lo_ref, carry_ref):
    @pl.when(pl.program_id(1) == 0)
    def _():
        carry_ref[...] = jnp.zeros_like(carry_ref)

    x = x_ref[0]
    h = (x * lax.rsqrt(jnp.mean(x * x, axis=-1, keepdims=True) + EPS) * g_ref[...]).astype(bf16)
    h_ref[0] = h
    f = lax.dot_general(h, wf_ref[...], (((1,), (1,)), ((), ())), preferred_element_type=f32) + bf_ref[...]
    lf = jnp.minimum(f, 0.0) - jnp.log1p(jnp.exp(-jnp.abs(f)))
    row = lax.broadcasted_iota(jnp.int32, (GATE_BM, GATE_BM), 0)
    col = lax.broadcasted_iota(jnp.int32, (GATE_BM, GATE_BM), 1)
    tri = (col <= row).astype(bf16)
    p0 = lf.astype(bf16)
    r0 = lf - p0.astype(f32)
    p1 = r0.astype(bf16)
    p2 = (r0 - p1.astype(f32)).astype(bf16)
    cs = (jnp.dot(tri, p0, preferred_element_type=f32)
          + jnp.dot(tri, p1, preferred_element_type=f32)
          + jnp.dot(tri, p2, preferred_element_type=f32))
    c = cs + carry_ref[...]
    carry_ref[...] = c[GATE_BM - 1:GATE_BM, :]
    c2 = c * LOG2E
    hi = c2.astype(bf16)
    r1 = c2 - hi.astype(f32)
    mid = r1.astype(bf16)
    lo = (r1 - mid.astype(f32)).astype(bf16)
    hi_ref[0] = hi
    mid_ref[0] = mid
    lo_ref[0] = lo


def fox_norm_gate(x3, g, wf, bfp):
    piece = jax.ShapeDtypeStruct((BATCH, SEQ, HEAD_DIM), bf16)
    spec = pl.BlockSpec((1, GATE_BM, HEAD_DIM), lambda b, s: (b, s, 0))
    rows = pl.BlockSpec((1, GATE_BM, D_MODEL), lambda b, s: (b, s, 0))
    return pl.pallas_call(
        _gate_kernel,
        out_shape=(jax.ShapeDtypeStruct((BATCH, SEQ, D_MODEL), bf16), piece, piece, piece),
        grid=(BATCH, SEQ // GATE_BM),
        in_specs=[rows,
                  pl.BlockSpec((1, D_MODEL), lambda b, s: (0, 0)),
                  pl.BlockSpec((HEAD_DIM, D_MODEL), lambda b, s: (0, 0)),
                  pl.BlockSpec((1, HEAD_DIM), lambda b, s: (0, 0))],
        out_specs=(rows, spec, spec, spec),
        scratch_shapes=[pltpu.VMEM((1, HEAD_DIM), f32)],
        compiler_params=_params(("arbitrary", "arbitrary"), 32),
        name="fox_norm_gate",
    )(x3, g.reshape(1, D_MODEL), wf, bfp)


PROJ_BM = 1024
PROJ_BN = 1024
PROJ_HEADS = PROJ_BN // HEAD_DIM
MXU_N = 256
FOX_BK = 256
FOX_V_ROWS = HEAD_DIM + 16


def _for_each_slab(a, w_ref, width, consume):
    n = w_ref.shape[1] // width

    def product(s):
        return jnp.dot(a, w_ref[:, s * width:(s + 1) * width], preferred_element_type=f32)

    y = product(0)
    for s in range(n):
        y_next = product(s + 1) if s + 1 < n else None
        consume(s, y)
        y = y_next


def _fox_proj_kernel(h_ref, w_ref, g_ref, o_ref, *, mode):
    if mode == "v":
        extra = lax.broadcasted_iota(jnp.int32, (FOX_V_ROWS - HEAD_DIM, FOX_BK), 0)
        ones_row = jnp.where(extra == 0, 1.0, 0.0).astype(bf16)

    def consume(s, y2):
        for u in range(MXU_N // HEAD_DIM):
            hh = s * (MXU_N // HEAD_DIM) + u
            y = y2[:, u * HEAD_DIM:(u + 1) * HEAD_DIM]
            if mode == "q":
                o_ref[0, hh] = _head_rmsnorm(y, g_ref[...]).T.astype(bf16)
            elif mode == "k":
                o_ref[0, hh] = _head_rmsnorm(y, g_ref[...]).astype(bf16)
            else:
                for c in range(PROJ_BM // FOX_BK):
                    o_ref[0, hh, c, :HEAD_DIM, :] = y[c * FOX_BK:(c + 1) * FOX_BK, :].T.astype(bf16)
                    o_ref[0, hh, c, HEAD_DIM:, :] = ones_row

    _for_each_slab(h_ref[0], w_ref, MXU_N, consume)


def fox_proj(h3, w, gain, mode, col_block0):
    nsb = SEQ // PROJ_BM
    if mode == "q":
        shape = (BATCH, FOX_HEADS, HEAD_DIM, SEQ)
        ospec = pl.BlockSpec((1, PROJ_HEADS, HEAD_DIM, PROJ_BM), lambda b, s, j: (b, j, 0, s))
    elif mode == "k":
        shape = (BATCH, FOX_HEADS, SEQ, HEAD_DIM)
        ospec = pl.BlockSpec((1, PROJ_HEADS, PROJ_BM, HEAD_DIM), lambda b, s, j: (b, j, s, 0))
    else:
        shape = (BATCH, FOX_HEADS, SEQ // FOX_BK, FOX_V_ROWS, FOX_BK)
        ospec = pl.BlockSpec((1, PROJ_HEADS, PROJ_BM // FOX_BK, FOX_V_ROWS, FOX_BK),
                             lambda b, s, j: (b, j, s, 0, 0))
    return pl.pallas_call(
        functools.partial(_fox_proj_kernel, mode=mode),
        out_shape=jax.ShapeDtypeStruct(shape, bf16),
        grid=(BATCH, nsb, D_MODEL // PROJ_BN),
        in_specs=[pl.BlockSpec((1, PROJ_BM, D_MODEL), lambda b, s, j: (b, s, 0)),
                  pl.BlockSpec((D_MODEL, PROJ_BN), lambda b, s, j: (0, col_block0 + j)),
                  pl.BlockSpec((1, HEAD_DIM), lambda b, s, j: (0, 0))],
        out_specs=ospec,
        compiler_params=_params(("arbitrary",) * 3, 48),
        name="fox_proj_" + mode,
    )(h3, w, gain)


FOX_BQ = 4096
FOX_TILES = FOX_BQ // FOX_BK
FOX_LEAD = 4
AUG = 2 * HEAD_DIM
AUG_ROWS = 16
AUG_CHUNK = 1024


def _fox_attn_kernel(qT_ref, chi_ref, cmid_ref, clo_ref, k_ref, vT_ref, o_ref,
                     kaug_ref, s_ref, m_ref, acc_ref):
    h = pl.program_id(1)
    qi = pl.program_id(2)
    pieces = (chi_ref, cmid_ref, clo_ref)
    n_p = len(pieces)

    @pl.when(qi == 0)
    def _():
        kaug_ref[:, :HEAD_DIM] = k_ref[0, 0]
        row = lax.broadcasted_iota(jnp.int32, (HEAD_DIM, HEAD_DIM), 0)
        col = lax.broadcasted_iota(jnp.int32, (HEAD_DIM, HEAD_DIM), 1)
        lane = lax.broadcasted_iota(jnp.int32, (AUG_CHUNK, HEAD_DIM), 1)
        for c in range(SEQ // AUG_CHUNK):
            rows = slice(c * AUG_CHUNK, (c + 1) * AUG_CHUNK)
            aug = jnp.where(lane < n_p, 1.0, 0.0)
            for p, ref in enumerate(pieces):
                sel = jnp.where((row == h) & (col == n_p + p), -1.0, 0.0).astype(bf16)
                aug = aug + jnp.dot(ref[0, rows, :], sel, preferred_element_type=f32)
            kaug_ref[rows, HEAD_DIM:] = aug.astype(bf16)

    q0 = pl.multiple_of(qi * FOX_BQ, FOX_BQ)
    row = lax.broadcasted_iota(jnp.int32, (AUG_ROWS, HEAD_DIM), 0)
    col = lax.broadcasted_iota(jnp.int32, (AUG_ROWS, HEAD_DIM), 1)
    slot = lax.broadcasted_iota(jnp.int32, (AUG_ROWS, FOX_BQ), 0)
    qa = jnp.where((slot >= n_p) & (slot < 2 * n_p), 1.0, 0.0)
    for p, ref in enumerate(pieces):
        sel = jnp.where((row == p) & (col == h), 1.0, 0.0).astype(bf16)
        qa = qa + lax.dot_general(sel, ref[0, pl.ds(q0, FOX_BQ), :], (((1,), (1,)), ((), ())),
                                  preferred_element_type=f32)
    q_aug = jnp.concatenate([qT_ref[0, 0], qa.astype(bf16),
                             jnp.zeros((HEAD_DIM - AUG_ROWS, FOX_BQ), bf16)], axis=0)

    def scores(kb, q):
        start = pl.multiple_of(kb * FOX_BK, FOX_BK)
        return jnp.dot(kaug_ref[pl.ds(start, FOX_BK), :], q, preferred_element_type=f32)

    def softmax_step(s, m):
        m_new = jnp.maximum(m, jnp.max(s, axis=0, keepdims=True))
        return m_new, jnp.exp2(m - m_new), jnp.exp2(s - m_new).astype(bf16)

    def pv(kb, p):
        return jnp.dot(vT_ref[0, 0, kb], p, preferred_element_type=f32)

    n_slabs = FOX_BQ // MXU_N
    slabs = [slice(n * MXU_N, (n + 1) * MXU_N) for n in range(n_slabs)]
    q_slabs = [q_aug[:, c] for c in slabs]
    kk = lax.broadcasted_iota(jnp.int32, (FOX_BK, MXU_N), 0)
    qq = lax.broadcasted_iota(jnp.int32, (FOX_BK, MXU_N), 1)

    def run_units(units, lookahead):
        m = [m_ref[:, c] for c in slabs]
        acc = [acc_ref[:, c] for c in slabs]
        todo = units + lookahead
        s_val = {i: s_ref[:, slabs[i]] for i in range(FOX_LEAD)}
        soft = {}
        for x in range(len(units) + 1):
            if x + FOX_LEAD < len(todo):
                kb, n, _ = todo[x + FOX_LEAD]
                s_val[x + FOX_LEAD] = scores(kb, q_slabs[n])
            if x >= 1:
                kb, n, _ = units[x - 1]
                alpha, p = soft.pop(x - 1)
                acc[n] = alpha * acc[n] + pv(kb, p)
            if x < len(units):
                kb, n, diagonal = units[x]
                s = s_val.pop(x)
                if diagonal:
                    s = jnp.where(kk <= qq, s, -MASKED)
                m[n], alpha, p = softmax_step(s, m[n])
                soft[x] = (alpha, p)
        for i in range(len(lookahead)):
            s_ref[:, slabs[i]] = s_val[len(units) + i]
        for n, c in enumerate(slabs):
            m_ref[:, c], acc_ref[:, c] = m[n], acc[n]

    for i in range(FOX_LEAD):
        s_ref[:, slabs[i]] = scores(0, q_slabs[i])
    m_ref[...] = jnp.full_like(m_ref, -MASKED)
    acc_ref[...] = jnp.zeros_like(acc_ref)

    def body(j, carry):
        kb0 = j * FOX_TILES
        run_units([(kb0 + u, n, False) for u in range(FOX_TILES) for n in range(n_slabs)],
                  [(kb0 + FOX_TILES, n, False) for n in range(FOX_LEAD)])
        return carry

    lax.fori_loop(0, qi, body, 0)

    kb0 = qi * FOX_TILES
    run_units([(kb0 + d, n, n == d) for d in range(FOX_TILES) for n in range(d, n_slabs)], [])
    o = acc_ref[:HEAD_DIM, :] / acc_ref[HEAD_DIM:HEAD_DIM + 1, :]
    o_ref[0] = o.T.astype(bf16)


def fox_attention(qT, c_pieces, k, vT):
    nq = SEQ // FOX_BQ
    nk = SEQ // FOX_BK
    cspec = pl.BlockSpec((1, SEQ, HEAD_DIM), lambda b, h, i: (b, 0, 0))
    return pl.pallas_call(
        _fox_attn_kernel,
        out_shape=jax.ShapeDtypeStruct((BATCH, SEQ, D_MODEL), bf16),
        grid=(BATCH, FOX_HEADS, nq),
        in_specs=[pl.BlockSpec((1, 1, HEAD_DIM, FOX_BQ), lambda b, h, i: (b, h, 0, i)),
                  cspec, cspec, cspec,
                  pl.BlockSpec((1, 1, SEQ, HEAD_DIM), lambda b, h, i: (b, h, 0, 0)),
                  pl.BlockSpec((1, 1, nk, FOX_V_ROWS, FOX_BK), lambda b, h, i: (b, h, 0, 0, 0))],
        out_specs=pl.BlockSpec((1, FOX_BQ, HEAD_DIM), lambda b, h, i: (b, i, h)),
        scratch_shapes=[pltpu.VMEM((SEQ, AUG), bf16),
                        pltpu.VMEM((FOX_BK, FOX_LEAD * MXU_N), f32),
                        pltpu.VMEM((1, FOX_BQ), f32),
                        pltpu.VMEM((FOX_V_ROWS, FOX_BQ), f32)],
        compiler_params=_params(("arbitrary",) * 3, 48),
        name="fox_attention",
    )(qT, *c_pieces, k, vT)


OUT_BM = 512
OUT_SLAB = 512
LANES = 128
DIL_QUARTER = OUT_BM // DIL_RES


def _rmsnorm_rows(y, g):
    ms = jnp.mean(y * y, axis=-1, keepdims=True)
    return y * lax.rsqrt(ms + EPS) * g


def _outproj_kernel(o_ref, w_ref, x_ref, g_ref, y_ref, h_ref):
    def consume(s, mix):
        cols = slice(s * OUT_SLAB, (s + 1) * OUT_SLAB)
        y_ref[:, cols] = x_ref[:, cols] + mix

    _for_each_slab(o_ref[...], w_ref, OUT_SLAB, consume)
    h_ref[...] = _rmsnorm_rows(y_ref[...], g_ref[...]).astype(bf16)


def _dil_outproj_kernel(o_ref, w_ref, x_ref, g_ref, y_ref, h_ref, nat_ref):
    per = MXU_N // LANES

    def consume(s, mix):
        for u in range(per):
            c = s * per + u
            for rho in range(DIL_RES):
                nat_ref[c, pl.ds(rho, DIL_QUARTER, stride=DIL_RES), :] = (
                    mix[rho * DIL_QUARTER:(rho + 1) * DIL_QUARTER, u * LANES:(u + 1) * LANES])
            cols = slice(c * LANES, (c + 1) * LANES)
            y_ref[:, cols] = x_ref[:, cols] + nat_ref[c]

    _for_each_slab(o_ref[0].reshape(OUT_BM, D_MODEL), w_ref, MXU_N, consume)
    h_ref[...] = _rmsnorm_rows(y_ref[...], g_ref[...]).astype(bf16)


def outproj_residual(o, w, x2d, g, residue_major):
    row = pl.BlockSpec((OUT_BM, D_MODEL), lambda i: (i, 0))
    if residue_major:
        quarters = DIL_TILE // OUT_BM
        o = o.reshape(TOKENS // DIL_TILE, DIL_RES, DIL_ROWS, D_MODEL)
        ospec = pl.BlockSpec((1, DIL_RES, DIL_QUARTER, D_MODEL), lambda i: (i // quarters, 0, i % quarters, 0))
        body, scratch = _dil_outproj_kernel, [pltpu.VMEM((D_MODEL // LANES, OUT_BM, LANES), f32)]
    else:
        ospec, body, scratch = row, _outproj_kernel, []
    return pl.pallas_call(
        body,
        out_shape=(jax.ShapeDtypeStruct((TOKENS, D_MODEL), f32), jax.ShapeDtypeStruct((TOKENS, D_MODEL), bf16)),
        grid=(TOKENS // OUT_BM,),
        in_specs=[ospec,
                  pl.BlockSpec((D_MODEL, D_MODEL), lambda i: (0, 0)),
                  row,
                  pl.BlockSpec((1, D_MODEL), lambda i: (0, 0))],
        out_specs=(row, row),
        scratch_shapes=scratch,
        compiler_params=_params(("arbitrary",), 52),
        name="dil_outproj_residual" if residue_major else "outproj_residual",
    )(o, w, x2d, g.reshape(1, D_MODEL))


MLP_BM = OUT_BM
MLP_BF = 1024


def _mlp_kernel(h_ref, wu_ref, wd_ref, x_ref, *refs, next_h):
    if next_h is None:
        (y_ref,) = refs
    else:
        g_ref, y_ref, hn_ref, slab_ref = refs
    f = pl.program_id(1)

    @pl.when(f == 0)
    def _():
        y_ref[...] = x_ref[...]

    a = jnp.maximum(jnp.dot(h_ref[...], wu_ref[...], preferred_element_type=f32), 0.0)
    y_ref[...] += jnp.dot((a * a).astype(bf16), wd_ref[...], preferred_element_type=f32)

    if next_h is not None:
        @pl.when(f == pl.num_programs(1) - 1)
        def _():
            hn = _rmsnorm_rows(y_ref[...], g_ref[...])
            for c in range(D_MODEL // LANES):
                slab_ref[c] = hn[:, c * LANES:(c + 1) * LANES]
            for rho in range(DIL_RES):
                for c in range(D_MODEL // LANES):
                    hn_ref[0, rho, :, c * LANES:(c + 1) * LANES] = (
                        slab_ref[c, pl.ds(rho, DIL_QUARTER, stride=DIL_RES), :].astype(bf16))


def mlp_residual(h2d, w_up, w_down, x2d, g_next=None):
    row = pl.BlockSpec((MLP_BM, D_MODEL), lambda i, f: (i, 0))
    in_specs = [row,
                pl.BlockSpec((D_MODEL, MLP_BF), lambda i, f: (0, f)),
                pl.BlockSpec((MLP_BF, D_MODEL), lambda i, f: (f, 0)),
                row]
    y_shape = jax.ShapeDtypeStruct((TOKENS, D_MODEL), f32)
    if g_next is None:
        args, out_shape, out_specs, scratch = (h2d, w_up, w_down, x2d), y_shape, row, []
    else:
        quarters = DIL_TILE // MLP_BM
        in_specs.append(pl.BlockSpec((1, D_MODEL), lambda i, f: (0, 0)))
        args = (h2d, w_up, w_down, x2d, g_next.reshape(1, D_MODEL))
        out_shape = (y_shape, jax.ShapeDtypeStruct((TOKENS // DIL_TILE, DIL_RES, DIL_ROWS, D_MODEL), bf16))
        out_specs = (row, pl.BlockSpec((1, DIL_RES, DIL_QUARTER, D_MODEL),
                                       lambda i, f: (i // quarters, 0, i % quarters, 0)))
        scratch = [pltpu.VMEM((D_MODEL // LANES, MLP_BM, LANES), f32)]
    return pl.pallas_call(
        functools.partial(_mlp_kernel, next_h=None if g_next is None else "residue_major"),
        out_shape=out_shape,
        grid=(TOKENS // MLP_BM, D_FF // MLP_BF),
        in_specs=in_specs,
        out_specs=out_specs,
        scratch_shapes=scratch,
        compiler_params=_params(("arbitrary", "arbitrary"), 56),
        name="mlp_residual",
    )(*args)


DIL_COLS = 2 * N_GROUPS * DIL_HEADS * HEAD_DIM + DIL_HEADS * DIL_V_DIM
DIL_QK_BLOCKS = 2 * N_GROUPS


def _dil_proj_kernel(h_ref, w_ref, g_ref, o_ref):
    j = pl.program_id(1)

    def slabs(normalise):
        def consume(s, y2):
            if normalise:
                for u in range(MXU_N // HEAD_DIM):
                    cols = slice(s * MXU_N + u * HEAD_DIM, s * MXU_N + (u + 1) * HEAD_DIM)
                    o_ref[:, cols] = _head_rmsnorm(y2[:, u * HEAD_DIM:(u + 1) * HEAD_DIM], g_ref[0]).astype(bf16)
            else:
                o_ref[:, s * MXU_N:(s + 1) * MXU_N] = y2.astype(bf16)

        _for_each_slab(h_ref[...], w_ref, MXU_N, consume)

    @pl.when(j < DIL_QK_BLOCKS)
    def _():
        slabs(True)

    @pl.when(j >= DIL_QK_BLOCKS)
    def _():
        slabs(False)


def dil_proj(h2d, w, gains):
    out = pl.pallas_call(
        _dil_proj_kernel,
        out_shape=jax.ShapeDtypeStruct((TOKENS, DIL_COLS), bf16),
        grid=(TOKENS // PROJ_BM, DIL_COLS // PROJ_BN),
        in_specs=[pl.BlockSpec((PROJ_BM, D_MODEL), lambda i, j: (i, 0)),
                  pl.BlockSpec((D_MODEL, PROJ_BN), lambda i, j: (0, j)),
                  pl.BlockSpec((1, 1, HEAD_DIM), lambda i, j: (j, 0, 0))],
        out_specs=pl.BlockSpec((PROJ_BM, PROJ_BN), lambda i, j: (i, j)),
        compiler_params=_params(("arbitrary", "arbitrary"), 48),
        name="dil_proj",
    )(h2d, w, gains)
    return out.reshape(BATCH, SEQ, DIL_COLS)


DIL_LEAD = 2


def _dil_distance_tables():
    def table(delta, prev_half, r):
        valid = (delta >= 0) & (delta <= DIL_SPAN)
        d = np.where(valid, delta * (r * LOG2E), MASKED).astype(np.float32)
        first = np.where(prev_half, np.float32(MASKED), d).astype(np.float32)
        return d, first

    i = np.arange(128)[:, None]
    j = np.arange(256)[None, :]
    t16 = table(128 + i - j, j < 128, 16)
    c, u = i // 32, i % 32
    cp, half, up = j // 64, (j % 64) // 32, j % 32
    t4 = table(128 + 4 * u + c - 128 * half - 4 * up - cp, half == 0, 4)
    i = np.arange(256)[:, None]
    j = np.arange(512)[None, :]
    rho, u = i // 16, i % 16
    rp, half, up = j // 32, (j % 32) // 16, j % 16
    t1 = table(256 + 16 * u + rho - 256 * half - 16 * up - rp, half == 0, 1)
    return t1, t4, t16


def _dil_attn_kernel(slopes_ref, q0_ref, q1_ref, q2_ref, k0_ref, k1_ref, k2_ref, v_ref,
                     d1_ref, d1f_ref, d4_ref, d4f_ref, d16_ref, d16f_ref,
                     o_ref, m_ref, l_ref, acc_ref):
    h = pl.program_id(1)
    t = pl.program_id(2)
    base = pl.multiple_of(t * DIL_TILE, DIL_TILE)
    pbase = pl.multiple_of(jnp.maximum(t - 1, 0) * DIL_TILE, DIL_TILE)
    first = t == 0

    def gather(ref, rows, size):
        return jnp.concatenate([ref[r:r + size, :] for r in rows], axis=0)

    def scores(u):
        q_ref, k_ref, bias, rows, size, n, _ = u
        q = jnp.concatenate([q_ref[0, r:r + size, :] for r in rows], axis=0)
        return lax.dot_general(q, window(k_ref, rows, size, n), (((1,), (1,)), ((), ())),
                               preferred_element_type=f32) + bias

    def softmax(u, s):
        _, _, _, rows, size, _, init = u
        m_new = jnp.max(s, axis=1, keepdims=True)
        m_old = None
        if not init:
            m_old = gather(m_ref, rows, size)
            m_new = jnp.maximum(m_old, m_new)
        p = jnp.exp2(s - m_new).astype(bf16)
        ones = jnp.ones((p.shape[1], LANES), bf16)
        return m_old, m_new, jnp.dot(p, ones, preferred_element_type=f32)[:, :1], p

    def accumulate(u, m_old, m_new, l_new, p):
        _, _, _, rows, size, n, init = u
        o_new = jnp.dot(p, window(v_ref, rows, size, n), preferred_element_type=f32)
        if not init:
            alpha = jnp.exp2(m_old - m_new)
            l_new = alpha * gather(l_ref, rows, size) + l_new
            o_new = alpha * gather(acc_ref, rows, size) + o_new
        for c, r in enumerate(rows):
            m_ref[r:r + size, :] = m_new[c * size:(c + 1) * size]
            l_ref[r:r + size, :] = l_new[c * size:(c + 1) * size]
            acc_ref[r:r + size, :] = o_new[c * size:(c + 1) * size]

    def window(ref, rows, size, n):
        if n > 0:
            return jnp.concatenate([ref[0, pl.ds(base + (r - size), 2 * size), :] for r in rows], axis=0)
        parts = []
        for r in rows:
            parts.append(ref[0, pl.ds(pbase + (r + DIL_ROWS - size), size), :])
            parts.append(ref[0, pl.ds(base + r, size), :])
        return jnp.concatenate(parts, axis=0)

    units = []
    slope = slopes_ref[2 * DIL_HEADS + h]
    bias = -slope * jnp.where(first, d16f_ref[...], d16_ref[...])
    for rho in range(DIL_RES):
        units.append((q2_ref, k2_ref, bias, [rho * DIL_ROWS], DIL_ROWS, 0, True))
    slope = slopes_ref[DIL_HEADS + h]
    bias_in = -slope * d4_ref[...]
    bias_edge = -slope * jnp.where(first, d4f_ref[...], d4_ref[...])
    for rho4 in range(4):
        for n in range(4):
            rows = [(rho4 + 4 * c) * DIL_ROWS + 32 * n for c in range(4)]
            units.append((q1_ref, k1_ref, bias_edge if n == 0 else bias_in, rows, 32, n, False))
    slope = slopes_ref[h]
    bias_in = -slope * d1_ref[...]
    bias_edge = -slope * jnp.where(first, d1f_ref[...], d1_ref[...])
    for n in range(DIL_ROWS // 16):
        rows = [rho * DIL_ROWS + 16 * n for rho in range(DIL_RES)]
        units.append((q0_ref, k0_ref, bias_edge if n == 0 else bias_in, rows, 16, n, False))

    s_val = {i: scores(units[i]) for i in range(DIL_LEAD)}
    soft = {}
    for x in range(len(units) + 1):
        if x + DIL_LEAD < len(units):
            s_val[x + DIL_LEAD] = scores(units[x + DIL_LEAD])
        if x >= 1:
            accumulate(units[x - 1], *soft.pop(x - 1))
        if x < len(units):
            soft[x] = softmax(units[x], s_val.pop(x))

    o_ref[0] = (acc_ref[...] / l_ref[...]).astype(bf16)


def dil_attention(p3, slopes):
    tables = [jnp.asarray(a) for pair in _dil_distance_tables() for a in pair]
    gh = DIL_HEADS
    kcol0 = N_GROUPS * gh
    vcol0 = 2 * N_GROUPS * gh * HEAD_DIM // DIL_V_DIM
    qspecs = [pl.BlockSpec((1, DIL_TILE, HEAD_DIM), lambda b, h, t, g=g: (b, t, g * gh + h))
              for g in range(N_GROUPS)]
    kspecs = [pl.BlockSpec((1, SEQ, HEAD_DIM), lambda b, h, t, g=g: (b, 0, kcol0 + g * gh + h))
              for g in range(N_GROUPS)]
    vspec = pl.BlockSpec((1, SEQ, DIL_V_DIM), lambda b, h, t: (b, 0, vcol0 + h))
    tspecs = [pl.BlockSpec(a.shape, lambda b, h, t: (0, 0)) for a in tables]
    return pl.pallas_call(
        _dil_attn_kernel,
        out_shape=jax.ShapeDtypeStruct((BATCH, SEQ, D_MODEL), bf16),
        grid=(BATCH, DIL_HEADS, SEQ // DIL_TILE),
        in_specs=[pl.BlockSpec(memory_space=pltpu.SMEM)] + qspecs + kspecs + [vspec] + tspecs,
        out_specs=pl.BlockSpec((1, DIL_TILE, DIL_V_DIM), lambda b, h, t: (b, t, h)),
        scratch_shapes=[pltpu.VMEM((DIL_TILE, 1), f32),
                        pltpu.VMEM((DIL_TILE, 1), f32),
                        pltpu.VMEM((DIL_TILE, DIL_V_DIM), f32)],
        compiler_params=_params(("arbitrary",) * 3, 48),
        name="dil_attention",
    )(slopes, p3, p3, p3, p3, p3, p3, p3, *tables)


def _fox_mixer(x2d, norm_g, w_in, b_f, q_gain, k_gain):
    wb, wf = to_bf16_split(w_in, 0, 3 * FOX_HEADS * HEAD_DIM)
    bfp = jnp.pad(b_f.astype(f32), (0, HEAD_DIM - FOX_HEADS)).reshape(1, HEAD_DIM)
    h3, *c_pieces = fox_norm_gate(x2d.reshape(BATCH, SEQ, D_MODEL), norm_g, wf, bfp)

    nb = D_MODEL // PROJ_BN
    qg = (q_gain * (QK_SCALE * LOG2E)).astype(f32).reshape(1, HEAD_DIM)
    kg = k_gain.astype(f32).reshape(1, HEAD_DIM)
    qT = fox_proj(h3, wb, qg, "q", 0)
    k = fox_proj(h3, wb, kg, "k", nb)
    vT = fox_proj(h3, wb, kg, "v", 2 * nb)
    return fox_attention(qT, c_pieces, k, vT).reshape(TOKENS, D_MODEL)


def _dil_mixer(h_rm, wb, q_gain, k_gain):
    ones = jnp.ones((DIL_COLS // PROJ_BN - DIL_QK_BLOCKS, HEAD_DIM), f32)
    gains = jnp.concatenate([q_gain.astype(f32) * (QK_SCALE * LOG2E), k_gain.astype(f32), ones], axis=0)
    p3 = dil_proj(h_rm, wb, gains.reshape(-1, 1, HEAD_DIM))
    n_heads = N_GROUPS * DIL_HEADS
    slopes = jnp.exp2(-ALIBI_MAX_EXP * jnp.arange(1, n_heads + 1, dtype=f32) / n_heads)
    return dil_attention(p3, slopes)


def kernel(x, fox_w_in, fox_b_f, fox_q_gain, fox_k_gain, fox_w_out, dil_w_in, dil_q_gain, dil_k_gain,
           dil_w_out, mix_norm_g, mlp_norm_g, mlp_w_up, mlp_w_down):
    x0 = x.reshape(TOKENS, D_MODEL)
    o = _fox_mixer(x0, mix_norm_g[0], fox_w_in, fox_b_f[0], fox_q_gain[0], fox_k_gain[0])
    x1, h = outproj_residual(o, to_bf16(fox_w_out, 0), x0, mlp_norm_g[0], residue_major=False)
    x2, h_rm = mlp_residual(h, to_bf16(mlp_w_up, 0), to_bf16(mlp_w_down, 0), x1, g_next=mix_norm_g[1])
    o = _dil_mixer(h_rm.reshape(TOKENS, D_MODEL), to_bf16(dil_w_in, 0), dil_q_gain[0], dil_k_gain[0])
    x3, h = outproj_residual(o, to_bf16(dil_w_out, 0), x2, mlp_norm_g[1], residue_major=True)
    x4 = mlp_residual(h, to_bf16(mlp_w_up, 1), to_bf16(mlp_w_down, 1), x3)
    return x4.reshape(BATCH, SEQ, D_MODEL)
```
